```python
import numpy as np
import jax
import jax.numpy as jnp
from jax import lax

D_MODEL = 2048
BATCH = 8
SEQ = 4096
DEPTH = 4

CTX_LEN = 256
GRID_W = 64
HEAD_DIM = 64
ROPE_BASE = 10000.0
EPS = 1e-6
F_MIN = 1e-6
NEG_INF = -1e30
Q_BLOCK = 128
A_HEADS = 8
A_KV_HEADS = 2
A_WINDOW = 128
B_HEADS = 8
B_NOPE = 64
B_ROPE = 32
B_V = 64
B_Q_RANK = 512
B_KV_RANK = 256
C_HEADS = 4
C_DK = 128
C_DV = 128
C_CHUNK = 64
D_HEADS = 8
NA_ROWS = 8
NA_COLS = 16
N_BRANCH = 4
BRANCH_W = 512
N_EXPERTS = 16
N_GROUPS = 4
TOP_K = 2
GROUP_SCORE_K = 2
D_EXPERT = 1024

IN_SIZES = (A_HEADS * HEAD_DIM, A_KV_HEADS * HEAD_DIM, A_KV_HEADS * HEAD_DIM,
            B_Q_RANK, B_KV_RANK, B_ROPE,
            C_HEADS * C_DK, C_HEADS * C_DV, C_HEADS * C_DK, C_HEADS * C_DK, C_HEADS * C_DV,
            D_HEADS * HEAD_DIM, D_HEADS * HEAD_DIM, D_HEADS * HEAD_DIM,
            N_BRANCH * D_MODEL)
IN_WIDTH = sum(IN_SIZES)

kernel_name = 'hybrid_prefix_dit_trunk'


def rms_norm(x, g):
    xf = x.astype(jnp.float32)
    y = xf * lax.rsqrt(jnp.mean(xf * xf, axis=-1, keepdims=True) + EPS)
    return (y * g.astype(jnp.float32)).astype(x.dtype)


def modulate(x, g, shift, scale):
    return rms_norm(x, g) * (1 + scale) + shift


def adaln(cvec, w, b):
    return jnp.split(jax.nn.silu(cvec) @ w + b, 6, axis=-1)


def axial_rope(n, rot_dim):
    n_freq = rot_dim // 4
    inv = jnp.asarray(ROPE_BASE ** (-np.arange(n_freq, dtype=np.float32) / n_freq), dtype=jnp.float32)
    t = jnp.arange(n, dtype=jnp.int32)
    row = (t // GRID_W).astype(jnp.float32)
    col = (t % GRID_W).astype(jnp.float32)
    ang = jnp.concatenate([row[:, None] * inv, col[:, None] * inv], axis=-1)
    return jnp.cos(ang), jnp.sin(ang)


def apply_rope(x, cos, sin):
    x1, x2 = jnp.split(x, 2, axis=-1)
    c = cos[None, :, None, :].astype(x.dtype)
    s = sin[None, :, None, :].astype(x.dtype)
    return jnp.concatenate([x1 * c - x2 * s, x1 * s + x2 * c], axis=-1)


def joint_softmax(parts, sink=None):
    sizes = [p.shape[-1] for p in parts]
    logits = list(parts)
    if sink is not None:
        logits.append(jnp.broadcast_to(sink, parts[0].shape[:-1] + (1,)))
    p = jax.nn.softmax(jnp.concatenate(logits, axis=-1), axis=-1)
    cuts = np.cumsum(sizes)[:-1].tolist()
    return jnp.split(p[..., :sum(sizes)], cuts, axis=-1)


def ctx_attention(qc, kc, vc, scale, sink=None):
    s = jnp.einsum('bqhd,bkhd->bhqk', qc, kc).astype(jnp.float32) * scale
    sk = None if sink is None else sink.astype(jnp.float32)[None, :, None, None]
    (p,) = joint_softmax([s], sk)
    o = jnp.einsum('bhqk,bkhd->bqhd', p.astype(vc.dtype), vc)
    return o.reshape(o.shape[0], o.shape[1], -1)


def mixer_window_gqa(q, k, v, qc, kc, vc, g_q, g_k, sink, rope, with_ctx_out):
    B, L, _ = q.shape
    grp = A_HEADS // A_KV_HEADS
    scale = HEAD_DIM ** -0.5

    def heads(t, n):
        return t.reshape(t.shape[0], t.shape[1], n, HEAD_DIM)

    q = apply_rope(rms_norm(heads(q, A_HEADS), g_q), *rope)
    k = apply_rope(rms_norm(heads(k, A_KV_HEADS), g_k), *rope)
    v = heads(v, A_KV_HEADS)
    qc = rms_norm(heads(qc, A_HEADS), g_q)
    kc = rms_norm(heads(kc, A_KV_HEADS), g_k)
    vc = heads(vc, A_KV_HEADS)
    band = Q_BLOCK + 2 * A_WINDOW
    nb = L // Q_BLOCK
    pad = ((0, 0), (A_WINDOW, A_WINDOW), (0, 0), (0, 0))
    kp = jnp.pad(k, pad)
    vp = jnp.pad(v, pad)
    q_blocks = q.reshape(B, nb, Q_BLOCK, A_KV_HEADS, grp, HEAD_DIM).swapaxes(0, 1)
    rel = np.arange(band)[None, :] - A_WINDOW - np.arange(Q_BLOCK)[:, None]
    in_window = jnp.asarray(np.abs(rel) <= A_WINDOW)
    sink_lat = sink.astype(jnp.float32).reshape(1, A_KV_HEADS, grp, 1, 1)

    def one_block(args):
        qb, bi = args
        start = bi * Q_BLOCK
        kb = lax.dynamic_slice_in_dim(kp, start, band, axis=1)
        vb = lax.dynamic_slice_in_dim(vp, start, band, axis=1)
        kpos = start - A_WINDOW + jnp.arange(band)
        ok = in_window & ((kpos >= 0) & (kpos < L))[None, :]
        s = jnp.einsum('bqkgd,bjkd->bkgqj', qb, kb).astype(jnp.float32) * scale
        s = jnp.where(ok, s, NEG_INF)
        sc = jnp.einsum('bqkgd,bjkd->bkgqj', qb, kc).astype(jnp.float32) * scale
        p, pc = joint_softmax([s, sc], sink_lat)
        o = (jnp.einsum('bkgqj,bjkd->bqkgd', p.astype(vb.dtype), vb)
             + jnp.einsum('bkgqj,bjkd->bqkgd', pc.astype(vc.dtype), vc))
        return o.reshape(B, Q_BLOCK, A_HEADS * HEAD_DIM)

    o = lax.map(one_block, (q_blocks, jnp.arange(nb)))
    o = o.swapaxes(0, 1).reshape(B, L, A_HEADS * HEAD_DIM)
    o_ctx = None
    if with_ctx_out:
        o_ctx = ctx_attention(qc, jnp.repeat(kc, grp, axis=2), jnp.repeat(vc, grp, axis=2), scale, sink)
    return o, o_ctx


def global_attention_with_ctx(q, k, v, kc, vc, scale):
    B, L, H, dq = q.shape
    nb = L // Q_BLOCK
    q_blocks = q.reshape(B, nb, Q_BLOCK, H, dq).swapaxes(0, 1)

    def one_block(qb):
        s = jnp.einsum('bqhd,bkhd->bhqk', qb, k).astype(jnp.float32) * scale
        sc = jnp.einsum('bqhd,bkhd->bhqk', qb, kc).astype(jnp.float32) * scale
        p, pc = joint_softmax([s, sc])
        return (jnp.einsum('bhqk,bkhd->bqhd', p.astype(v.dtype), v)
                + jnp.einsum('bhqk,bkhd->bqhd', pc.astype(vc.dtype), vc))

    o = lax.map(one_block, q_blocks)
    return o.swapaxes(0, 1).reshape(B, L, H * v.shape[-1])


def mixer_mla(cq, ckv, kr, cq_c, ckv_c, kr_c, g_qa, g_kva, w_q_up, w_kv_up, g_q, g_k, rope, with_ctx_out):
    scale = (B_NOPE + B_ROPE) ** -0.5

    def project(cq, ckv, kr, rope):
        Bn, n, _ = cq.shape
        q = (rms_norm(cq, g_qa) @ w_q_up).reshape(Bn, n, B_HEADS, B_NOPE + B_ROPE)
        kv = (rms_norm(ckv, g_kva) @ w_kv_up).reshape(Bn, n, B_HEADS, B_NOPE + B_V)
        q_nope = rms_norm(q[..., :B_NOPE], g_q[:B_NOPE])
        q_rope = rms_norm(q[..., B_NOPE:], g_q[B_NOPE:])
        k_nope = rms_norm(kv[..., :B_NOPE], g_k[:B_NOPE])
        k_rope = rms_norm(kr, g_k[B_NOPE:])[:, :, None, :]
        if rope is not None:
            q_rope = apply_rope(q_rope, *rope)
            k_rope = apply_rope(k_rope, *rope)
        k_rope = jnp.broadcast_to(k_rope, (Bn, n, B_HEADS, B_ROPE))
        q = jnp.concatenate([q_nope, q_rope], axis=-1)
        k = jnp.concatenate([k_nope, k_rope], axis=-1)
        return q, k, kv[..., B_NOPE:]

    q, k, v = project(cq, ckv, kr, rope)
    qc, kc, vc = project(cq_c, ckv_c, kr_c, None)
    o = global_attention_with_ctx(q, k, v, kc, vc, scale)
    o_ctx = ctx_attention(qc, kc, vc, scale) if with_ctx_out else None
    return o, o_ctx


def gla_scan(q, k, v, logf, s0):
    B, n, H, DK = q.shape
    DV = v.shape[-1]
    nc = n // C_CHUNK

    def chunks(t):
        return t.astype(jnp.float32).reshape(B, nc, C_CHUNK, H, t.shape[-1]).transpose(1, 0, 3, 2, 4)

    causal = jnp.tril(jnp.ones((C_CHUNK, C_CHUNK), dtype=bool))[:, :, None]

    def step(S, inp):
        qc, kc, vc, gc = inp
        b = jnp.cumsum(gc, axis=2)
        diff = b[:, :, :, None, :] - b[:, :, None, :, :]
        decay = jnp.where(causal, jnp.exp(jnp.where(causal, diff, 0.0)), 0.0)
        A = jnp.einsum('bhid,bhijd,bhjd->bhij', qc, decay, kc)
        o = jnp.einsum('bhij,bhjv->bhiv', A, vc) + jnp.einsum('bhid,bhdv->bhiv', qc * jnp.exp(b), S)
        b_last = b[:, :, -1, :]
        S = (jnp.exp(b_last)[..., None] * S
             + jnp.einsum('bhjd,bhjv->bhdv', kc * jnp.exp(b_last[:, :, None, :] - b), vc))
        return S, o

    S, o = lax.scan(step, s0, (chunks(q), chunks(k), chunks(v), chunks(logf)))
    o = o.transpose(1, 0, 3, 2, 4).reshape(B, n, H, DV)
    return o, S


def mixer_hgrn2(q, i, f_fw, f_bw, g, qc, ic, fc_fw, fc_bw, gc, lb_fw, lb_bw, g_out, with_ctx_out):
    def heads(t):
        return t.reshape(t.shape[0], t.shape[1], C_HEADS, -1)

    def forget(f, lb):
        lb = lb.astype(jnp.float32)
        fg = lb + (1.0 - lb) * jax.nn.sigmoid(f.astype(jnp.float32))
        logf = jnp.log(jnp.maximum(fg, F_MIN))
        return heads(logf), heads(1.0 - fg)

    def flip(t):
        return t[:, ::-1]

    def bidir(q, v, f_fw, f_bw, s_fw, s_bw):
        qh = heads(jax.nn.silu(q))
        vh = heads(v)
        lf_f, k_f = forget(f_fw, lb_fw)
        lf_b, k_b = forget(f_bw, lb_bw)
        o_f, S_f = gla_scan(qh, k_f, vh, lf_f, s_fw)
        o_b, S_b = gla_scan(flip(qh), flip(k_b), flip(vh), flip(lf_b), s_bw)
        return o_f + flip(o_b), S_f, S_b

    zeros = jnp.zeros((q.shape[0], C_HEADS, C_DK, C_DV), jnp.float32)
    o_c, Sc_f, Sc_b = bidir(qc, ic, fc_fw, fc_bw, zeros, zeros)
    o, _, _ = bidir(q, i, f_fw, f_bw, Sc_f, Sc_b)

    def readout(o, gate):
        out = rms_norm(o, g_out) * jax.nn.silu(heads(gate).astype(jnp.float32))
        return out.reshape(o.shape[0], o.shape[1], C_HEADS * C_DV).astype(gate.dtype)

    return readout(o, g), (readout(o_c, gc) if with_ctx_out else None)


def mixer_neighborhood(q, k, v, qc, kc, vc, g_q, g_k, rpb, with_ctx_out):
    B, L, _ = q.shape
    rows = L // GRID_W
    kr = min(NA_ROWS, rows)
    scale = HEAD_DIM ** -0.5

    def heads(t):
        return t.reshape(t.shape[0], t.shape[1], D_HEADS, HEAD_DIM)

    q = rms_norm(heads(q), g_q)
    k = rms_norm(heads(k), g_k)
    v = heads(v)
    qc = rms_norm(heads(qc), g_q)
    kc = rms_norm(heads(kc), g_k)
    vc = heads(vc)
    n_cb = GRID_W // NA_COLS
    band_c = 2 * NA_COLS
    cb = np.arange(n_cb)
    band_start = np.clip(cb * NA_COLS - NA_COLS // 2, 0, GRID_W - band_c)
    key_cols = band_start[:, None] + np.arange(band_c)[None, :]
    q_cols = cb[:, None] * NA_COLS + np.arange(NA_COLS)[None, :]
    win_start = np.clip(q_cols - NA_COLS // 2, 0, GRID_W - NA_COLS)
    kcol = key_cols[:, None, :]
    col_ok = (kcol >= win_start[..., None]) & (kcol < win_start[..., None] + NA_COLS)
    mask = jnp.asarray(col_ok)[:, :, None, :]
    dc = kcol - q_cols[:, :, None]
    col_bias = rpb[:, :, dc + NA_COLS - 1]
    kg = k.reshape(B, rows, GRID_W, D_HEADS, HEAD_DIM)
    vg = v.reshape(B, rows, GRID_W, D_HEADS, HEAD_DIM)
    q_rows = q.reshape(B, rows, n_cb, NA_COLS, D_HEADS, HEAD_DIM).swapaxes(0, 1)

    def one_row(args):
        qr, r = args
        r0 = jnp.clip(r - kr // 2, 0, rows - kr)
        kband = lax.dynamic_slice_in_dim(kg, r0, kr, axis=1)[:, :, key_cols]
        vband = lax.dynamic_slice_in_dim(vg, r0, kr, axis=1)[:, :, key_cols]
        s = jnp.einsum('bcqhd,brckhd->bhcqrk', qr, kband).astype(jnp.float32) * scale
        dr = r0 + jnp.arange(kr) - r + NA_ROWS - 1
        bias = col_bias[:, dr].transpose(0, 2, 3, 1, 4).astype(jnp.float32)
        s = jnp.where(mask, s + bias, NEG_INF).reshape(B, D_HEADS, n_cb, NA_COLS, kr * band_c)
        sc = jnp.einsum('bcqhd,bkhd->bhcqk', qr, kc).astype(jnp.float32) * scale
        p, pc = joint_softmax([s, sc])
        p = p.reshape(B, D_HEADS, n_cb, NA_COLS, kr, band_c).astype(vband.dtype)
        o = (jnp.einsum('bhcqrk,brckhd->bcqhd', p, vband)
             + jnp.einsum('bhcqk,bkhd->bcqhd', pc.astype(vc.dtype), vc))
        return o.reshape(B, GRID_W, D_HEADS, HEAD_DIM)

    o = lax.map(one_row, (q_rows, jnp.arange(rows)))
    o = o.swapaxes(0, 1).reshape(B, L, D_HEADS * HEAD_DIM)
    o_ctx = ctx_attention(qc, kc, vc, scale) if with_ctx_out else None
    return o, o_ctx


def merge_branches(outs, gates, w_branch, w_out):
    g = jnp.split(gates, N_BRANCH, axis=-1)
    y = jax.nn.sigmoid(g[0]) * (outs[0] @ w_branch[0])
    for b in range(1, N_BRANCH):
        y = y + jax.nn.sigmoid(g[b]) * (outs[b] @ w_branch[b])
    return y @ w_out


def token_mixers(h, hc, lp, lb_fw, lb_bw, rope_a, rope_b, with_ctx_out):
    cuts = np.cumsum(IN_SIZES)[:-1].tolist()
    (a_q, a_k, a_v, b_cq, b_ckv, b_kr, c_q, c_i, c_ff, c_fb, c_g,
     d_q, d_k, d_v, gates) = jnp.split(h @ lp['w_in'], cuts, axis=-1)
    (a_qc, a_kc, a_vc, b_cqc, b_ckvc, b_krc, c_qc, c_ic, c_ffc, c_fbc, c_gc,
     d_qc, d_kc, d_vc, gates_c) = jnp.split(hc @ lp['w_in'], cuts, axis=-1)
    oa, oa_c = mixer_window_gqa(a_q, a_k, a_v, a_qc, a_kc, a_vc, lp['a_gq'], lp['a_gk'], lp['a_sink'],
                                rope_a, with_ctx_out)
    ob, ob_c = mixer_mla(b_cq, b_ckv, b_kr, b_cqc, b_ckvc, b_krc, lp['b_g_qa'], lp['b_g_kva'],
                         lp['b_w_q_up'], lp['b_w_kv_up'], lp['b_gq'], lp['b_gk'], rope_b, with_ctx_out)
    oc, oc_c = mixer_hgrn2(c_q, c_i, c_ff, c_fb, c_g, c_qc, c_ic, c_ffc, c_fbc, c_gc,
                           lb_fw, lb_bw, lp['c_g_out'], with_ctx_out)
    od, od_c = mixer_neighborhood(d_q, d_k, d_v, d_qc, d_kc, d_vc, lp['d_gq'], lp['d_gk'], lp['d_rpb'],
                                  with_ctx_out)
    y = merge_branches([oa, ob, oc, od], gates, lp['w_branch'], lp['w_out'])
    yc = None
    if with_ctx_out:
        yc = merge_branches([oa_c, ob_c, oc_c, od_c], gates_c, lp['w_branch'], lp['w_out'])
    return y, yc


def moe_ffn(h, router_w, router_b, w_gate, w_up, w_down):
    shp = h.shape
    t = h.reshape(-1, shp[-1])
    scores = jax.nn.sigmoid(t.astype(jnp.float32) @ router_w.astype(jnp.float32))
    biased = scores + router_b.astype(jnp.float32)
    per_group = N_EXPERTS // N_GROUPS
    grp_score = lax.top_k(biased.reshape(-1, N_GROUPS, per_group), GROUP_SCORE_K)[0].sum(-1)
    best = jnp.argmax(grp_score, axis=-1)
    in_group = (jnp.arange(N_EXPERTS) // per_group)[None, :] == best[:, None]
    _, idx = lax.top_k(jnp.where(in_group, biased, NEG_INF), TOP_K)
    w = jnp.take_along_axis(scores, idx, axis=-1)
    w = w / jnp.sum(w, axis=-1, keepdims=True)
    gates = jnp.sum(jax.nn.one_hot(idx, N_EXPERTS, dtype=jnp.float32) * w[..., None], axis=1)
    y = jnp.zeros(t.shape, jnp.float32)
    for e in range(N_EXPERTS):
        he = jax.nn.silu(t @ w_gate[e]) * (t @ w_up[e])
        y = y + gates[:, e:e + 1] * (he @ w_down[e]).astype(jnp.float32)
    return y.astype(h.dtype).reshape(shp)


def setup_inputs(seed: int = 0) -> dict:
    key = jax.random.key(seed)
    ks = jax.random.split(key, 32)
    f32 = jnp.float32
    D = D_MODEL

    def nrm(k, shape, s):
        return jax.random.normal(k, shape, f32) * s

    def gain(k, shape):
        return 1.0 + 0.02 * jax.random.normal(k, shape, f32)

    return {
        'x': nrm(ks[0], (BATCH, SEQ, D), 1.0),
        'c': nrm(ks[1], (BATCH, D), 1.0),
        'ctx': nrm(ks[2], (BATCH, CTX_LEN, D), 1.0),
        'c_ctx': nrm(ks[3], (D,), 1.0),
        'w_mod': nrm(ks[4], (DEPTH, D, 6 * D), 0.5 * D ** -0.5),
        'b_mod': nrm(ks[5], (DEPTH, 6 * D), 0.02),
        'g_norm_mix': gain(ks[6], (DEPTH, D)),
        'g_norm_ffn': gain(ks[7], (DEPTH, D)),
        'w_in': nrm(ks[8], (DEPTH, D, IN_WIDTH), D ** -0.5),
        'a_gq': gain(ks[9], (DEPTH, HEAD_DIM)),
        'a_gk': gain(ks[10], (DEPTH, HEAD_DIM)),
        'a_sink': nrm(ks[11], (DEPTH, A_HEADS), 0.5),
        'b_g_qa': gain(ks[12], (DEPTH, B_Q_RANK)),
        'b_g_kva': gain(ks[13], (DEPTH, B_KV_RANK)),
        'b_w_q_up': nrm(ks[14], (DEPTH, B_Q_RANK, B_HEADS * (B_NOPE + B_ROPE)), B_Q_RANK ** -0.5),
        'b_w_kv_up': nrm(ks[15], (DEPTH, B_KV_RANK, B_HEADS * (B_NOPE + B_V)), B_KV_RANK ** -0.5),
        'b_gq': gain(ks[16], (DEPTH, B_NOPE + B_ROPE)),
        'b_gk': gain(ks[17], (DEPTH, B_NOPE + B_ROPE)),
        'c_lb_logits': nrm(ks[18], (2, DEPTH, C_HEADS * C_DK), 0.1),
        'c_g_out': gain(ks[19], (DEPTH, C_DV)),
        'd_gq': gain(ks[20], (DEPTH, HEAD_DIM)),
        'd_gk': gain(ks[21], (DEPTH, HEAD_DIM)),
        'd_rpb': nrm(ks[22], (DEPTH, D_HEADS, 2 * NA_ROWS - 1, 2 * NA_COLS - 1), 0.1),
        'w_branch': nrm(ks[23], (DEPTH, N_BRANCH, BRANCH_W, D), BRANCH_W ** -0.5),
        'w_out': nrm(ks[24], (DEPTH, D, D), D ** -0.5),
        'router_w': nrm(ks[25], (D, N_EXPERTS), D ** -0.5),
        'router_b': nrm(ks[26], (N_EXPERTS,), 0.01),
        'w_exp_gate': nrm(ks[27], (DEPTH, N_EXPERTS, D, D_EXPERT), D ** -0.5),
        'w_exp_up': nrm(ks[28], (DEPTH, N_EXPERTS, D, D_EXPERT), D ** -0.5),
        'w_exp_down': nrm(ks[29], (DEPTH, N_EXPERTS, D_EXPERT, D), D_EXPERT ** -0.5),
    }


def reference(x, c, ctx, c_ctx, w_mod, b_mod, g_norm_mix, g_norm_ffn, w_in, a_gq, a_gk, a_sink,
              b_g_qa, b_g_kva, b_w_q_up, b_w_kv_up, b_gq, b_gk, c_lb_logits, c_g_out, d_gq, d_gk, d_rpb,
              w_branch, w_out, router_w, router_b, w_exp_gate, w_exp_up, w_exp_down):
    L = x.shape[1]
    rope_a = axial_rope(L, HEAD_DIM)
    rope_b = axial_rope(L, B_ROPE)
    lb_p = jax.nn.softmax(c_lb_logits.astype(jnp.float32), axis=1)
    lb_all = jnp.clip(jnp.cumsum(lb_p, axis=1) - lb_p[:, :1], 0.0, 1.0)
    xc = ctx
    for l in range(DEPTH):
        with_ctx_out = l < DEPTH - 1
        lp = {
            'w_in': w_in[l], 'a_gq': a_gq[l], 'a_gk': a_gk[l], 'a_sink': a_sink[l],
            'b_g_qa': b_g_qa[l], 'b_g_kva': b_g_kva[l], 'b_w_q_up': b_w_q_up[l], 'b_w_kv_up': b_w_kv_up[l],
            'b_gq': b_gq[l], 'b_gk': b_gk[l], 'c_g_out': c_g_out[l], 'd_gq': d_gq[l], 'd_gk': d_gk[l],
            'd_rpb': d_rpb[l], 'w_branch': w_branch[l], 'w_out': w_out[l],
        }
        m = [t[:, None, :] for t in adaln(c, w_mod[l], b_mod[l])]
        mc = adaln(c_ctx, w_mod[l], b_mod[l])
        h = modulate(x, g_norm_mix[l], m[0], m[1])
        hc = modulate(xc, g_norm_mix[l], mc[0], mc[1])
        y, yc = token_mixers(h, hc, lp, lb_all[0, l], lb_all[1, l], rope_a, rope_b, with_ctx_out)
        x = x + m[2] * y
        h2 = modulate(x, g_norm_ffn[l], m[3], m[4])
        x = x + m[5] * moe_ffn(h2, router_w, router_b, w_exp_gate[l], w_exp_up[l], w_exp_down[l])
        if with_ctx_out:
            xc = xc + mc[2] * yc
            hc2 = modulate(xc, g_norm_ffn[l], mc[3], mc[4])
            xc = xc + mc[5] * moe_ffn(hc2, router_w, router_b, w_exp_gate[l], w_exp_up[l], w_exp_down[l])
    return x
```

```python
import functools

import numpy as np
import jax
import jax.numpy as jnp
from jax import lax
from jax.experimental import pallas as pl
from jax.experimental.pallas import tpu as pltpu

F32 = jnp.float32
BF16 = jnp.bfloat16
HIGHEST = lax.Precision.HIGHEST

GRID_W = 64
HEAD_DIM = 64
ROPE_BASE = 10000.0
EPS = 1e-6
F_MIN = 1e-6
NEG_INF = -1e30
A_HEADS, A_KV_HEADS, A_WINDOW = 8, 2, 128
B_HEADS, B_NOPE, B_ROPE, B_V, B_Q_RANK, B_KV_RANK = 8, 64, 32, 64, 512, 256
C_HEADS, C_DK, C_DV = 4, 128, 128
D_HEADS, NA_ROWS, NA_COLS = 8, 8, 16
N_BRANCH, BRANCH_W = 4, 512
N_EXPERTS, N_GROUPS, TOP_K, D_EXPERT = 16, 4, 2, 1024

LANES = 128
VMEM_LIMIT = 56 * 1024 * 1024

COL_AQ, COL_AK, COL_AV, COL_BCKV, COL_BCQ = 0, 512, 640, 768, 1024
COL_CQ, COL_CI, COL_CFF, COL_CFB, COL_CG = 1536, 2048, 2560, 3072, 3584
COL_DQ, COL_DK, COL_DV, COL_GATES = 4096, 4608, 5120, 5632
PROJ_W = COL_GATES + N_BRANCH * 2048

Q_TILE = 256
GLA_CHUNK = 128
CHUNK_ROWS = 16
EXPERT_TILE = 256


def _cparams(sem, vmem=VMEM_LIMIT):
    return pltpu.CompilerParams(dimension_semantics=sem, vmem_limit_bytes=vmem)


def _sigmoid(x):
    return 1.0 / (1.0 + jnp.exp(-x))


def _silu(x):
    return x * _sigmoid(x)


def _rot_pairs(x, half):
    n = x.shape[-1]
    ax = x.ndim - 1
    lane = lax.broadcasted_iota(jnp.int32, x.shape, ax)
    fwd = pltpu.roll(x, n - half, axis=ax)
    bwd = pltpu.roll(x, half, axis=ax)
    return jnp.where((lane & (2 * half - 1)) < half, fwd, bwd)


def _row_mod(i, mb_ref, mc_ref, idx, tile, tiles_per_batch, lc):
    row = (i % tiles_per_batch) * tile + lax.broadcasted_iota(jnp.int32, (tile, 1), 0)
    return jnp.where(row < lc, mc_ref[0, idx:idx + 1, :], mb_ref[0, idx:idx + 1, :])


def _mod_norm(x, g, shift, scale):
    ms = jnp.mean(x * x, axis=-1, keepdims=True)
    return x * lax.rsqrt(ms + EPS) * g * (1.0 + scale) + shift


def _adaln_kernel(c_ref, w_ref, b_ref, o_ref):
    cv = c_ref[...]
    o_ref[0] = jnp.dot(_silu(cv), w_ref[0], preferred_element_type=F32, precision=HIGHEST) + b_ref[0]


def _adaln(cvecs, w_mod, b_mod):
    depth, d, n6 = w_mod.shape
    rows = cvecs.shape[0]
    tn = 1024
    return pl.pallas_call(
        _adaln_kernel,
        grid=(depth, n6 // tn),
        in_specs=[pl.BlockSpec((rows, d), lambda l, j: (0, 0)),
                  pl.BlockSpec((1, d, tn), lambda l, j: (l, 0, j)),
                  pl.BlockSpec((1, 1, tn), lambda l, j: (l, 0, j))],
        out_specs=pl.BlockSpec((1, rows, tn), lambda l, j: (l, 0, j)),
        out_shape=jax.ShapeDtypeStruct((depth, rows, n6), F32),
        compiler_params=_cparams(("parallel", "parallel")),
        name="adaln",
    )(cvecs, w_mod, b_mod.reshape(depth, 1, n6))


def _inproj_kernel(x_ref, mb_ref, mc_ref, g_ref, w_ref, wkr_ref, o_ref, kr_ref, h_scr, *, tile, tpb, lc):
    i = pl.program_id(0)

    @pl.when(pl.program_id(1) == 0)
    def _():
        shift = _row_mod(i, mb_ref, mc_ref, 0, tile, tpb, lc)
        scale = _row_mod(i, mb_ref, mc_ref, 1, tile, tpb, lc)
        hb = _mod_norm(x_ref[...], g_ref[...], shift, scale).astype(BF16)
        h_scr[...] = hb
        kr_ref[...] = jnp.dot(hb, wkr_ref[...], preferred_element_type=F32).astype(BF16)

    o_ref[...] = jnp.dot(h_scr[...], w_ref[...], preferred_element_type=F32).astype(BF16)


def _in_proj(xs, mods, g, w, wkr, nb, s, lc):
    ntok, d = xs.shape
    tile = s // 4
    tpb = s // tile
    tn = 512
    kern = functools.partial(_inproj_kernel, tile=tile, tpb=tpb, lc=lc)
    return pl.pallas_call(
        kern,
        grid=(ntok // tile, PROJ_W // tn),
        in_specs=[pl.BlockSpec((tile, d), lambda i, j: (i, 0)),
                  pl.BlockSpec((1, 6, d), lambda i, j: (i // tpb, 0, 0)),
                  pl.BlockSpec((1, 6, d), lambda i, j: (nb, 0, 0)),
                  pl.BlockSpec((1, d), lambda i, j: (0, 0)),
                  pl.BlockSpec((d, tn), lambda i, j: (0, j)),
                  pl.BlockSpec((d, LANES), lambda i, j: (0, 0))],
        out_specs=[pl.BlockSpec((tile, tn), lambda i, j: (i, j)),
                   pl.BlockSpec((tile, LANES), lambda i, j: (i, 0))],
        out_shape=[jax.ShapeDtypeStruct((ntok, PROJ_W), BF16),
                   jax.ShapeDtypeStruct((ntok, LANES), BF16)],
        scratch_shapes=[pltpu.VMEM((tile, d), BF16)],
        compiler_params=_cparams(("parallel", "arbitrary")),
        name="in_proj",
    )(xs, mods, mods, g.reshape(1, d), w, wkr)


def _group_norm(x, gmat, inv_size, gvec):
    ss = jnp.dot((x * x).astype(BF16), gmat, preferred_element_type=F32)
    return x * lax.rsqrt(ss * inv_size + EPS) * gvec


def _prep_ad_kernel(a_ref, d_ref, cos_ref, sin_ref, g_ref, gmat_ref, o_ref):
    gmat = gmat_ref[...]
    cos = cos_ref[...]
    sin = sin_ref[...]
    scale = HEAD_DIM ** -0.5
    inv = 1.0 / HEAD_DIM
    for blk in range(5):
        x = a_ref[:, blk * LANES:(blk + 1) * LANES].astype(F32)
        gi = 0 if blk < 4 else 1
        xn = _group_norm(x, gmat, inv, g_ref[gi:gi + 1, :])
        xr = xn * cos + _rot_pairs(xn, HEAD_DIM // 2) * sin
        if blk < 4:
            xr = xr * scale
        ob = blk if blk < 4 else 12
        o_ref[:, ob * LANES:(ob + 1) * LANES] = xr.astype(BF16)
    for blk in range(8):
        x = d_ref[:, blk * LANES:(blk + 1) * LANES].astype(F32)
        gi = 2 if blk < 4 else 3
        xn = _group_norm(x, gmat, inv, g_ref[gi:gi + 1, :])
        if blk < 4:
            xn = xn * scale
        o_ref[:, (4 + blk) * LANES:(5 + blk) * LANES] = xn.astype(BF16)


def _prep_ad(proj, cos, sin, gvecs, gmat, s):
    ntok = proj.shape[0]
    tile = s // 4
    tpb = s // tile
    wa, wd = 5 * LANES, 8 * LANES
    return pl.pallas_call(
        _prep_ad_kernel,
        grid=(ntok // tile,),
        in_specs=[pl.BlockSpec((tile, wa), lambda i: (i, COL_AQ // wa)),
                  pl.BlockSpec((tile, wd), lambda i: (i, COL_DQ // wd)),
                  pl.BlockSpec((tile, LANES), lambda i: (i % tpb, 0)),
                  pl.BlockSpec((tile, LANES), lambda i: (i % tpb, 0)),
                  pl.BlockSpec((4, LANES), lambda i: (0, 0)),
                  pl.BlockSpec((LANES, LANES), lambda i: (0, 0))],
        out_specs=pl.BlockSpec((tile, 13 * LANES), lambda i: (i, 0)),
        out_shape=jax.ShapeDtypeStruct((ntok, 13 * LANES), BF16),
        compiler_params=_cparams(("parallel",)),
        name="prep_ad",
    )(proj, proj, cos, sin, gvecs, gmat)


def _prep_b_kernel(cq_ref, ckv_ref, kr_ref, wq_ref, wk_ref, wv_ref, gqa_ref, gkva_ref, gq_ref, gk_ref,
                   gmat_ref, inv_ref, cos_ref, sin_ref, q_ref, k_ref, v_ref):
    gmat = gmat_ref[...]
    inv = inv_ref[...]
    cos = cos_ref[...]
    sin = sin_ref[...]
    scale = (B_NOPE + B_ROPE) ** -0.5

    def rms(x, g):
        return x * lax.rsqrt(jnp.mean(x * x, axis=-1, keepdims=True) + EPS) * g

    cq = rms(cq_ref[...].astype(F32), gqa_ref[...]).astype(BF16)
    ckv = rms(ckv_ref[...].astype(F32), gkva_ref[...]).astype(BF16)
    q = jnp.dot(cq, wq_ref[...], preferred_element_type=F32)
    kn = jnp.dot(ckv, wk_ref[...], preferred_element_type=F32)
    v_ref[...] = jnp.dot(ckv, wv_ref[...], preferred_element_type=F32).astype(BF16)
    kro = _group_norm(kr_ref[...].astype(F32), gmat, inv, gk_ref[...])
    kro = kro * cos + _rot_pairs(kro, B_ROPE // 2) * sin
    for h in range(B_HEADS):
        sl = slice(h * LANES, (h + 1) * LANES)
        qh = _group_norm(q[:, sl], gmat, inv, gq_ref[...])
        qh = (qh * cos + _rot_pairs(qh, B_ROPE // 2) * sin) * scale
        q_ref[:, sl] = qh.astype(BF16)
        kh = _group_norm(kn[:, sl], gmat, inv, gk_ref[...]) + kro
        k_ref[:, sl] = kh.astype(BF16)


def _prep_b(proj, kr, wq, wk, wv, gqa, gkva, gq, gk, gmat, inv, cos, sin, s):
    ntok = proj.shape[0]
    tile = s // 4
    tpb = s // tile
    hw = B_HEADS * LANES
    full = lambda shape: pl.BlockSpec(shape, lambda i: (0, 0))
    return pl.pallas_call(
        _prep_b_kernel,
        grid=(ntok // tile,),
        in_specs=[pl.BlockSpec((tile, B_Q_RANK), lambda i: (i, COL_BCQ // B_Q_RANK)),
                  pl.BlockSpec((tile, B_KV_RANK), lambda i: (i, COL_BCKV // B_KV_RANK)),
                  pl.BlockSpec((tile, LANES), lambda i: (i, 0)),
                  full((B_Q_RANK, hw)), full((B_KV_RANK, hw)), full((B_KV_RANK, B_HEADS * B_V)),
                  full((1, B_Q_RANK)), full((1, B_KV_RANK)), full((1, LANES)), full((1, LANES)),
                  full((LANES, LANES)), full((1, LANES)),
                  pl.BlockSpec((tile, LANES), lambda i: (i % tpb, 0)),
                  pl.BlockSpec((tile, LANES), lambda i: (i % tpb, 0))],
        out_specs=[pl.BlockSpec((tile, hw), lambda i: (i, 0)),
                   pl.BlockSpec((tile, hw), lambda i: (i, 0)),
                   pl.BlockSpec((tile, B_HEADS * B_V), lambda i: (i, 0))],
        out_shape=[jax.ShapeDtypeStruct((ntok, hw), BF16),
                   jax.ShapeDtypeStruct((ntok, hw), BF16),
                   jax.ShapeDtypeStruct((ntok, B_HEADS * B_V), BF16)],
        compiler_params=_cparams(("parallel",)),
        name="prep_b",
    )(proj, proj, kr, wq, wk, wv, gqa, gkva, gq, gk, gmat, inv, cos, sin)


def _nt_dot(a, b):
    return lax.dot_general(a, b, (((1,), (1,)), ((), ())), preferred_element_type=F32)


def _half_mask(shape, half):
    lane = lax.broadcasted_iota(jnp.int32, shape, len(shape) - 1)
    return (lane >= HEAD_DIM) if half else (lane < HEAD_DIM)


def _joint_attend(s, sc, vw, vc, sink=None):
    m = jnp.maximum(jnp.max(s, axis=-1, keepdims=True), jnp.max(sc, axis=-1, keepdims=True))
    if sink is not None:
        m = jnp.maximum(m, sink)
    p = jnp.exp(s - m)
    pc = jnp.exp(sc - m)
    den = jnp.sum(p, axis=-1, keepdims=True) + jnp.sum(pc, axis=-1, keepdims=True)
    if sink is not None:
        den = den + jnp.exp(sink - m)
    o = (jnp.dot(p.astype(BF16), vw, preferred_element_type=F32)
         + jnp.dot(pc.astype(BF16), vc, preferred_element_type=F32))
    return o / den


def _attn_a_kernel(sink_ref, q_ref, k_ref, v_ref, o_ref, *, lc, l):
    qt = pl.program_id(1)
    span = Q_TILE + 2 * A_WINDOW
    qstart = (qt - 1) * Q_TILE
    ws = jnp.clip(qstart - A_WINDOW, 0, l - span)
    wrow = pl.multiple_of(lc + ws, LANES)
    kw = k_ref[pl.ds(wrow, span), :]
    vw = v_ref[pl.ds(wrow, span), :]
    kc = k_ref[0:lc, :]
    vc = v_ref[0:lc, :]
    kpos = ws + lax.broadcasted_iota(jnp.int32, (Q_TILE, span), 1)
    qoff = jnp.where(qt > 0, qstart, -(1 << 20))
    qpos = qoff + lax.broadcasted_iota(jnp.int32, (Q_TILE, span), 0)
    ok = jnp.abs(kpos - qpos) <= A_WINDOW
    for blk in range(4):
        qb = q_ref[:, blk * LANES:(blk + 1) * LANES]
        halves = []
        for half in range(2):
            head = blk + 4 * half
            qm = jnp.where(_half_mask(qb.shape, half), qb, jnp.zeros_like(qb))
            s = jnp.where(ok, _nt_dot(qm, kw), NEG_INF)
            sc = _nt_dot(qm, kc)
            halves.append(_joint_attend(s, sc, vw, vc, sink_ref[head]))
        out = jnp.where(_half_mask(halves[0].shape, 0), halves[0], halves[1])
        o_ref[:, blk * LANES:(blk + 1) * LANES] = out.astype(BF16)


def _attn_a(sink, qk, proj, nb, s, lc):
    ntok = qk.shape[0]
    nq = s // Q_TILE
    kern = functools.partial(_attn_a_kernel, lc=lc, l=s - lc)
    return pl.pallas_call(
        kern,
        grid_spec=pltpu.PrefetchScalarGridSpec(
            num_scalar_prefetch=0,
            grid=(nb, nq),
            in_specs=[pl.BlockSpec(memory_space=pltpu.SMEM),
                      pl.BlockSpec((Q_TILE, 4 * LANES), lambda b, t: (b * nq + t, 0)),
                      pl.BlockSpec((s, LANES), lambda b, t: (b, 12)),
                      pl.BlockSpec((s, LANES), lambda b, t: (b, COL_AV // LANES))],
            out_specs=pl.BlockSpec((Q_TILE, 4 * LANES), lambda b, t: (b * nq + t, 0)),
        ),
        out_shape=jax.ShapeDtypeStruct((ntok, 4 * LANES), BF16),
        compiler_params=_cparams(("parallel", "arbitrary")),
        name="attn_a",
    )(sink, qk, qk, proj)


def _attn_b_kernel(q_ref, k_ref, v_ref, o_ref, *, s):
    qt = pl.program_id(2)
    nchunks = jnp.where(qt == 0, 1, s // Q_TILE)
    res = []
    for hh in range(2):
        q = q_ref[:, hh * LANES:(hh + 1) * LANES]

        def body(c, carry, hh=hh, q=q):
            m, l, acc = carry
            row = pl.multiple_of(c * Q_TILE, Q_TILE)
            k = k_ref[pl.ds(row, Q_TILE), hh * LANES:(hh + 1) * LANES]
            v = v_ref[pl.ds(row, Q_TILE), :]
            sc = _nt_dot(q, k)
            m_new = jnp.maximum(m, jnp.max(sc, axis=-1, keepdims=True))
            alpha = jnp.exp(m - m_new)
            p = jnp.exp(sc - m_new)
            l = alpha * l + jnp.sum(p, axis=-1, keepdims=True)
            acc = alpha * acc + jnp.dot(p.astype(BF16), v, preferred_element_type=F32)
            return m_new, l, acc

        init = (jnp.full((Q_TILE, 1), NEG_INF, F32), jnp.zeros((Q_TILE, 1), F32),
                jnp.zeros((Q_TILE, LANES), F32))
        m, l, acc = lax.fori_loop(0, nchunks, body, init)
        res.append(acc / l)
    out = jnp.where(_half_mask(res[0].shape, 0), res[0], res[1])
    o_ref[...] = out.astype(BF16)


def _attn_b(qb, kb, vb, nb, s):
    ntok = qb.shape[0]
    nq = s // Q_TILE
    kern = functools.partial(_attn_b_kernel, s=s)
    return pl.pallas_call(
        kern,
        grid=(nb, B_HEADS // 2, nq),
        in_specs=[pl.BlockSpec((Q_TILE, 2 * LANES), lambda b, p, t: (b * nq + t, p)),
                  pl.BlockSpec((s, 2 * LANES), lambda b, p, t: (b, p)),
                  pl.BlockSpec((s, LANES), lambda b, p, t: (b, p))],
        out_specs=pl.BlockSpec((Q_TILE, LANES), lambda b, p, t: (b * nq + t, p)),
        out_shape=jax.ShapeDtypeStruct((ntok, B_HEADS * B_V), BF16),
        compiler_params=_cparams(("parallel", "parallel", "arbitrary")),
        name="attn_b",
    )(qb, kb, vb)


NA_QROWS = Q_TILE // GRID_W
NA_KROWS = NA_QROWS + NA_ROWS
NA_KEYS = NA_KROWS * GRID_W


def _attn_d_kernel(q_ref, k_ref, v_ref, bias_ref, o_ref, *, lc, rows):
    qt = pl.program_id(1)
    ks = jnp.clip((qt - 1) * NA_QROWS - NA_ROWS // 2, 0, rows - NA_KROWS)
    wrow = pl.multiple_of(lc + ks * GRID_W, Q_TILE)
    for blk in range(4):
        sl = slice(blk * LANES, (blk + 1) * LANES)
        qb = q_ref[:, sl]
        kw = k_ref[pl.ds(wrow, NA_KEYS), sl]
        vw = v_ref[pl.ds(wrow, NA_KEYS), sl]
        kc = k_ref[0:lc, sl]
        vc = v_ref[0:lc, sl]
        halves = []
        for half in range(2):
            head = 2 * blk + half
            qm = jnp.where(_half_mask(qb.shape, half), qb, jnp.zeros_like(qb))
            s = _nt_dot(qm, kw) + bias_ref[0, head]
            sc = _nt_dot(qm, kc)
            halves.append(_joint_attend(s, sc, vw, vc))
        out = jnp.where(_half_mask(halves[0].shape, 0), halves[0], halves[1])
        o_ref[:, sl] = out.astype(BF16)


def _na_variant(t, nq):
    return jnp.where(t == 0, 3, jnp.where(t == 1, 0, jnp.where(t == nq - 1, 2, 1)))


def _attn_d(qk, proj, bias, nb, s, lc):
    ntok = qk.shape[0]
    nq = s // Q_TILE
    rows = (s - lc) // GRID_W
    kern = functools.partial(_attn_d_kernel, lc=lc, rows=rows)
    w = 4 * LANES
    return pl.pallas_call(
        kern,
        grid=(nb, nq),
        in_specs=[pl.BlockSpec((Q_TILE, w), lambda b, t: (b * nq + t, 1)),
                  pl.BlockSpec((s, w), lambda b, t: (b, 2)),
                  pl.BlockSpec((s, w), lambda b, t: (b, COL_DV // w)),
                  pl.BlockSpec((1, D_HEADS, Q_TILE, NA_KEYS), lambda b, t: (_na_variant(t, nq), 0, 0, 0))],
        out_specs=pl.BlockSpec((Q_TILE, w), lambda b, t: (b * nq + t, 0)),
        out_shape=jax.ShapeDtypeStruct((ntok, w), BF16),
        compiler_params=_cparams(("parallel", "arbitrary")),
        name="attn_d",
    )(qk, qk, proj, bias)


def _cumsum_rows(x, reverse):
    n = x.shape[0]
    row = lax.broadcasted_iota(jnp.int32, x.shape, 0)
    step = 1
    while step < n:
        if reverse:
            x = x + jnp.where(row < n - step, pltpu.roll(x, n - step, axis=0), 0.0)
        else:
            x = x + jnp.where(row >= step, pltpu.roll(x, step, axis=0), 0.0)
        step *= 2
    return x


def _block_ref(b, size, idx):
    n, w = b.shape
    b3 = b.reshape(n // size, size, w)
    return jnp.broadcast_to(b3[:, idx:idx + 1, :], b3.shape).reshape(n, w)


def _gla_chunk(q, v, fpre, lb, st_ref, reverse):
    c = q.shape[0]
    fg = lb + (1.0 - lb) * _sigmoid(fpre)
    logf = jnp.log(jnp.maximum(fg, F_MIN))
    k = 1.0 - fg
    qs = _silu(q)
    b = _cumsum_rows(logf, reverse)
    row = lax.broadcasted_iota(jnp.int32, (c, c), 0)
    col = lax.broadcasted_iota(jnp.int32, (c, c), 1)
    xor = row ^ col
    after = (row < col) if reverse else (row > col)

    base = 8
    bref = _block_ref(b, base, base // 2 if reverse else base // 2 - 1)
    qe = (qs * jnp.exp(b - bref)).astype(BF16)
    ke = (k * jnp.exp(bref - b)).astype(BF16)
    a = jnp.where((xor < base) & (after | (row == col)), _nt_dot(qe, ke), 0.0)
    m = base
    while m < c:
        bref = _block_ref(b, 2 * m, m if reverse else m - 1)
        qe = (qs * jnp.exp(jnp.minimum(b - bref, 0.0))).astype(BF16)
        ke = (k * jnp.exp(jnp.minimum(bref - b, 0.0))).astype(BF16)
        a = jnp.where((xor >= m) & (xor < 2 * m) & after, _nt_dot(qe, ke), a)
        m *= 2

    st = st_ref[...]
    vb = v.astype(BF16)
    o = (jnp.dot(a.astype(BF16), vb, preferred_element_type=F32)
         + _nt_dot((qs * jnp.exp(b)).astype(BF16), st.astype(BF16)))
    btot = b[0:1, :] if reverse else b[c - 1:c, :]
    kdec = (k * jnp.exp(btot - b)).astype(BF16)
    upd = lax.dot_general(vb, kdec, (((0,), (0,)), ((), ())), preferred_element_type=F32)
    st_ref[...] = st * jnp.exp(btot) + upd
    return o


def _gla_kernel(qf_ref, vf_ref, ff_ref, qb_ref, vb_ref, fb_ref, lb_ref, of_ref, ob_ref, stf, stb):
    @pl.when(pl.program_id(2) == 0)
    def _():
        stf[...] = jnp.zeros_like(stf)
        stb[...] = jnp.zeros_like(stb)

    of_ref[...] = _gla_chunk(qf_ref[...].astype(F32), vf_ref[...].astype(F32), ff_ref[...].astype(F32),
                             lb_ref[0:1, :], stf, False)
    ob_ref[...] = _gla_chunk(qb_ref[...].astype(F32), vb_ref[...].astype(F32), fb_ref[...].astype(F32),
                             lb_ref[1:2, :], stb, True)


def _gla(proj, lb, nb, s, lc):
    ntok = proj.shape[0]
    c = GLA_CHUNK
    ns = s // c
    nctx = lc // c

    def bw_chunk(t):
        return jnp.where(t < nctx, nctx - 1 - t, ns - 1 - (t - nctx))

    def fw(col):
        return pl.BlockSpec((c, LANES), lambda b, h, t: (b * ns + t, col // LANES + h))

    def bw(col):
        return pl.BlockSpec((c, LANES), lambda b, h, t: (b * ns + bw_chunk(t), col // LANES + h))

    return pl.pallas_call(
        _gla_kernel,
        grid=(nb, C_HEADS, ns),
        in_specs=[fw(COL_CQ), fw(COL_CI), fw(COL_CFF), bw(COL_CQ), bw(COL_CI), bw(COL_CFB),
                  pl.BlockSpec((2, LANES), lambda b, h, t: (0, h))],
        out_specs=[pl.BlockSpec((c, LANES), lambda b, h, t: (b * ns + t, h)),
                   pl.BlockSpec((c, LANES), lambda b, h, t: (b * ns + bw_chunk(t), h))],
        out_shape=[jax.ShapeDtypeStruct((ntok, C_HEADS * C_DV), F32)] * 2,
        scratch_shapes=[pltpu.VMEM((C_DV, C_DK), F32), pltpu.VMEM((C_DV, C_DK), F32)],
        compiler_params=_cparams(("parallel", "parallel", "arbitrary")),
        name="gla",
    )(proj, proj, proj, proj, proj, proj, lb)


def _merge_kernel(oa_ref, ob_ref, ofw_ref, obw_ref, cg_ref, od_ref, g0_ref, g1_ref, g2_ref, g3_ref, x_ref,
                  mb_ref, mc_ref, wb_ref, wo_ref, gout_ref, gffn_ref, xn_ref, h2_ref, oc_scr,
                  *, tile, tpb, lc):
    i = pl.program_id(0)
    n = pl.program_id(1)

    @pl.when(n == 0)
    def _():
        gout = gout_ref[...]
        for h in range(C_HEADS):
            sl = slice(h * LANES, (h + 1) * LANES)
            o = ofw_ref[:, sl] + obw_ref[:, sl]
            on = o * lax.rsqrt(jnp.mean(o * o, axis=-1, keepdims=True) + EPS) * gout
            oc_scr[:, sl] = (on * _silu(cg_ref[:, sl].astype(F32))).astype(BF16)

    outs = (oa_ref[...], ob_ref[...], oc_scr[...], od_ref[...])
    gates = (g0_ref, g1_ref, g2_ref, g3_ref)
    y = None
    for br in range(N_BRANCH):
        t = _sigmoid(gates[br][...].astype(F32)) * jnp.dot(outs[br], wb_ref[br], preferred_element_type=F32)
        y = t if y is None else y + t
    part = jnp.dot(y.astype(BF16), wo_ref[...], preferred_element_type=F32)

    @pl.when(n == 0)
    def _():
        xn_ref[...] = part

    @pl.when(n > 0)
    def _():
        xn_ref[...] += part

    @pl.when(n == pl.num_programs(1) - 1)
    def _():
        xn = x_ref[...] + _row_mod(i, mb_ref, mc_ref, 2, tile, tpb, lc) * xn_ref[...]
        xn_ref[...] = xn
        h2 = _mod_norm(xn, gffn_ref[...], _row_mod(i, mb_ref, mc_ref, 3, tile, tpb, lc),
                       _row_mod(i, mb_ref, mc_ref, 4, tile, tpb, lc))
        for j in range(CHUNK_ROWS):
            h2_ref[pl.ds(j, tile, stride=CHUNK_ROWS), :] = h2[:, j * LANES:(j + 1) * LANES]


def _merge(oa, ob, ofw, obw, od, proj, xs, mods, wb, wo, gout, gffn, nb, s, lc):
    ntok, d = xs.shape
    tile = s // 16
    tpb = s // tile
    tn = 512
    nn = d // tn
    kern = functools.partial(_merge_kernel, tile=tile, tpb=tpb, lc=lc)
    row512 = pl.BlockSpec((tile, 512), lambda i, n: (i, 0))

    def gate(br):
        return pl.BlockSpec((tile, tn), lambda i, n: (i, (COL_GATES + br * d) // tn + n))

    return pl.pallas_call(
        kern,
        grid=(ntok // tile, nn),
        in_specs=[row512, row512, row512, row512,
                  pl.BlockSpec((tile, 512), lambda i, n: (i, COL_CG // 512)),
                  row512, gate(0), gate(1), gate(2), gate(3),
                  pl.BlockSpec((tile, d), lambda i, n: (i, 0)),
                  pl.BlockSpec((1, 6, d), lambda i, n: (i // tpb, 0, 0)),
                  pl.BlockSpec((1, 6, d), lambda i, n: (nb, 0, 0)),
                  pl.BlockSpec((N_BRANCH, BRANCH_W, tn), lambda i, n: (0, 0, n)),
                  pl.BlockSpec((tn, d), lambda i, n: (n, 0)),
                  pl.BlockSpec((1, LANES), lambda i, n: (0, 0)),
                  pl.BlockSpec((1, d), lambda i, n: (0, 0))],
        out_specs=[pl.BlockSpec((tile, d), lambda i, n: (i, 0)),
                   pl.BlockSpec((tile * CHUNK_ROWS, LANES), lambda i, n: (i, 0))],
        out_shape=[jax.ShapeDtypeStruct((ntok, d), F32),
                   jax.ShapeDtypeStruct((ntok * CHUNK_ROWS, LANES), F32)],
        scratch_shapes=[pltpu.VMEM((tile, 512), BF16)],
        compiler_params=_cparams(("parallel", "arbitrary")),
        name="merge",
    )(oa, ob, ofw, obw, proj, od, proj, proj, proj, proj, xs, mods, mods, wb, wo, gout, gffn)


def _router_kernel(h_ref, rw_ref, rb_ref, idx_ref, w_ref, *, tile):
    logits = None
    for j in range(CHUNK_ROWS):
        hj = h_ref[pl.ds(j, tile, stride=CHUNK_ROWS), :]
        part = lax.dot_general(rw_ref[:, j * LANES:(j + 1) * LANES], hj, (((1,), (1,)), ((), ())),
                               preferred_element_type=F32, precision=HIGHEST)
        logits = part if logits is None else logits + part
    scores = _sigmoid(logits)
    biased = scores + rb_ref[...]
    sc = [scores[e:e + 1, :] for e in range(N_EXPERTS)]
    bi = [biased[e:e + 1, :] for e in range(N_EXPERTS)]
    per = N_EXPERTS // N_GROUPS
    best_g = jnp.zeros((1, tile), jnp.int32)
    best_s = None
    for g in range(N_GROUPS):
        members = bi[g * per:(g + 1) * per]
        gs = None
        for a in range(per):
            for b in range(a + 1, per):
                pair = members[a] + members[b]
                gs = pair if gs is None else jnp.maximum(gs, pair)
        if best_s is None:
            best_s = gs
        else:
            better = gs > best_s
            best_g = jnp.where(better, g, best_g)
            best_s = jnp.where(better, gs, best_s)
    picks = []
    taken = jnp.full((1, tile), -1, jnp.int32)
    for _ in range(TOP_K):
        bv = jnp.full((1, tile), -3e38, F32)
        be = jnp.zeros((1, tile), jnp.int32)
        bw = jnp.zeros((1, tile), F32)
        for e in range(N_EXPERTS):
            val = jnp.where(best_g == e // per, bi[e], NEG_INF)
            better = (val > bv) & (taken != e)
            bv = jnp.where(better, val, bv)
            be = jnp.where(better, e, be)
            bw = jnp.where(better, sc[e], bw)
        picks.append((be, bw))
        taken = be
    tot = picks[0][1] + picks[1][1]
    idx_ref[...] = jnp.concatenate([picks[0][0], picks[1][0]], axis=0)
    w_ref[...] = jnp.concatenate([picks[0][1] / tot, picks[1][1] / tot], axis=0)


def _router(h2c, rw_t, rb):
    ntok = h2c.shape[0] // CHUNK_ROWS
    tile = 1024 if ntok % 1024 == 0 else 256
    d = rw_t.shape[1]
    kern = functools.partial(_router_kernel, tile=tile)
    return pl.pallas_call(
        kern,
        grid=(ntok // tile,),
        in_specs=[pl.BlockSpec((tile * CHUNK_ROWS, LANES), lambda i: (i, 0)),
                  pl.BlockSpec((N_EXPERTS, d), lambda i: (0, 0)),
                  pl.BlockSpec((N_EXPERTS, 1), lambda i: (0, 0))],
        out_specs=[pl.BlockSpec((TOP_K, tile), lambda i: (0, i)),
                   pl.BlockSpec((TOP_K, tile), lambda i: (0, i))],
        out_shape=[jax.ShapeDtypeStruct((TOP_K, ntok), jnp.int32),
                   jax.ShapeDtypeStruct((TOP_K, ntok), F32)],
        compiler_params=_cparams(("parallel",)),
        name="router",
    )(h2c, rw_t, rb)


def _expert_kernel(texp_ref, tval_ref, ent_ref, h_hbm, rw_ref, wg_ref, wu_ref, wd_ref, y_hbm,
                   gbuf, xbuf, obuf, gsem, ssem, *, ntok, tile):
    i = pl.program_id(0)
    base = i * tile

    def row_copy_in(r):
        e = ent_ref[base + r]
        src = jnp.where(e >= 2 * ntok, 0, jnp.where(e >= ntok, e - ntok, e))
        return pltpu.make_async_copy(
            h_hbm.at[pl.ds(pl.multiple_of(src * CHUNK_ROWS, CHUNK_ROWS), CHUNK_ROWS), :],
            gbuf.at[pl.ds(pl.multiple_of(r * CHUNK_ROWS, CHUNK_ROWS), CHUNK_ROWS), :], gsem)

    def row_copy_out(r):
        e = ent_ref[base + r]
        return pltpu.make_async_copy(
            obuf.at[pl.ds(pl.multiple_of(r * CHUNK_ROWS, CHUNK_ROWS), CHUNK_ROWS), :],
            y_hbm.at[pl.ds(pl.multiple_of(e * CHUNK_ROWS, CHUNK_ROWS), CHUNK_ROWS), :], ssem)

    def start_in(r, carry):
        row_copy_in(r).start()
        return carry

    def wait_in(r, carry):
        row_copy_in(r).wait()
        return carry

    def start_out(r, carry):
        row_copy_out(r).start()
        return carry

    def wait_out(r, carry):
        row_copy_out(r).wait()
        return carry

    @pl.when(tval_ref[i] > 0)
    def _():
        lax.fori_loop(0, tile, start_in, 0)
        lax.fori_loop(0, tile, wait_in, 0)
        for j in range(CHUNK_ROWS):
            xbuf[:, j * LANES:(j + 1) * LANES] = gbuf[pl.ds(j, tile, stride=CHUNK_ROWS), :].astype(BF16)
        x = xbuf[...]
        hg = jnp.dot(x, wg_ref[0], preferred_element_type=F32)
        hu = jnp.dot(x, wu_ref[0], preferred_element_type=F32)
        he = (_silu(hg) * hu).astype(BF16)
        y = jnp.dot(he, wd_ref[0], preferred_element_type=F32) * rw_ref[...]
        for j in range(CHUNK_ROWS):
            obuf[pl.ds(j, tile, stride=CHUNK_ROWS), :] = y[:, j * LANES:(j + 1) * LANES]

    @pl.when(tval_ref[i] == 0)
    def _():
        obuf[...] = jnp.zeros_like(obuf)

    lax.fori_loop(0, tile, start_out, 0)
    lax.fori_loop(0, tile, wait_out, 0)


def _experts(h2c, tile_exp, tile_valid, entries, row_w, wg, wu, wd, ntok):
    tile = EXPERT_TILE
    p = entries.shape[0]
    d = wg.shape[1]
    de = wg.shape[2]
    kern = functools.partial(_expert_kernel, ntok=ntok, tile=tile)
    return pl.pallas_call(
        kern,
        grid_spec=pltpu.PrefetchScalarGridSpec(
            num_scalar_prefetch=3,
            grid=(p // tile,),
            in_specs=[pl.BlockSpec(memory_space=pl.ANY),
                      pl.BlockSpec((tile, 1), lambda i, te, tv, en: (i, 0)),
                      pl.BlockSpec((1, d, de), lambda i, te, tv, en: (te[i], 0, 0)),
                      pl.BlockSpec((1, d, de), lambda i, te, tv, en: (te[i], 0, 0)),
                      pl.BlockSpec((1, de, d), lambda i, te, tv, en: (te[i], 0, 0))],
            out_specs=pl.BlockSpec(memory_space=pl.ANY),
            scratch_shapes=[pltpu.VMEM((tile * CHUNK_ROWS, LANES), F32),
                            pltpu.VMEM((tile, d), BF16),
                            pltpu.VMEM((tile * CHUNK_ROWS, LANES), F32),
                            pltpu.SemaphoreType.DMA(()),
                            pltpu.SemaphoreType.DMA(())]),
        out_shape=jax.ShapeDtypeStruct((p * CHUNK_ROWS, LANES), F32),
        compiler_params=_cparams(("arbitrary",)),
        name="experts",
    )(tile_exp, tile_valid, entries, h2c, row_w, wg, wu, wd)


def _route_plan(idx, w, ntok):
    tile = EXPERT_TILE
    n_ent = TOP_K * ntok
    p = n_ent + N_EXPERTS * tile
    flat_e = idx.reshape(-1)
    onehot = (flat_e[:, None] == jnp.arange(N_EXPERTS)[None, :]).astype(jnp.int32)
    rank = jnp.cumsum(onehot, axis=0) - onehot
    counts = jnp.sum(onehot, axis=0)
    padded = ((counts + tile - 1) // tile) * tile
    gstart = jnp.cumsum(padded) - padded
    pos = jnp.sum(onehot * (gstart[None, :] + rank), axis=1)
    ent = jnp.arange(n_ent, dtype=jnp.int32)
    assigned = jnp.zeros((p,), jnp.int32).at[pos].set(1)
    pad_id = n_ent + jnp.cumsum(1 - assigned) - 1
    entries = jnp.where(assigned == 1, jnp.zeros((p,), jnp.int32).at[pos].set(ent), pad_id).astype(jnp.int32)
    row_w = jnp.zeros((p,), F32).at[pos].set(w.reshape(-1)).reshape(p, 1)
    tstart = jnp.arange(p // tile, dtype=jnp.int32) * tile
    gend = gstart + padded
    texp = jnp.sum((tstart[:, None] >= gend[None, :]).astype(jnp.int32), axis=1)
    used = tstart < gend[-1]
    texp = jnp.where(used, texp, N_EXPERTS - 1).astype(jnp.int32)
    tval = used.astype(jnp.int32)
    return texp, tval, entries, row_w


def _combine_kernel(x_ref, y0_ref, y1_ref, mb_ref, mc_ref, o_ref, *, tile, tpb, lc):
    gate = _row_mod(pl.program_id(0), mb_ref, mc_ref, 5, tile, tpb, lc)
    for j in range(CHUNK_ROWS):
        sl = slice(j * LANES, (j + 1) * LANES)
        y = y0_ref[pl.ds(j, tile, stride=CHUNK_ROWS), :] + y1_ref[pl.ds(j, tile, stride=CHUNK_ROWS), :]
        o_ref[:, sl] = x_ref[:, sl] + gate[:, sl] * y


def _combine(xn, ybuf, mods, nb, s, lc):
    ntok, d = xn.shape
    tile = s // 16
    tpb = s // tile
    nt = ntok // tile
    kern = functools.partial(_combine_kernel, tile=tile, tpb=tpb, lc=lc)
    return pl.pallas_call(
        kern,
        grid=(nt,),
        in_specs=[pl.BlockSpec((tile, d), lambda i: (i, 0)),
                  pl.BlockSpec((tile * CHUNK_ROWS, LANES), lambda i: (i, 0)),
                  pl.BlockSpec((tile * CHUNK_ROWS, LANES), lambda i: (nt + i, 0)),
                  pl.BlockSpec((1, 6, d), lambda i: (i // tpb, 0, 0)),
                  pl.BlockSpec((1, 6, d), lambda i: (nb, 0, 0))],
        out_specs=pl.BlockSpec((tile, d), lambda i: (i, 0)),
        out_shape=jax.ShapeDtypeStruct((ntok, d), F32),
        compiler_params=_cparams(("parallel",)),
        name="combine",
    )(xn, ybuf, ybuf, mods, mods)


def _rope_tables(l, lc, rot_dim, lane_off, group):
    n_freq = rot_dim // 4
    inv = jnp.asarray(ROPE_BASE ** (-np.arange(n_freq, dtype=np.float32) / n_freq), dtype=F32)
    t = jnp.arange(l, dtype=jnp.int32)
    row = (t // GRID_W).astype(F32)
    col = (t % GRID_W).astype(F32)
    ang = jnp.concatenate([row[:, None] * inv, col[:, None] * inv], axis=-1)
    cos, sin = jnp.cos(ang), jnp.sin(ang)
    cos_g = jnp.ones((l, group), F32).at[:, lane_off:lane_off + rot_dim].set(jnp.concatenate([cos, cos], -1))
    sin_g = jnp.zeros((l, group), F32).at[:, lane_off:lane_off + rot_dim].set(jnp.concatenate([-sin, sin], -1))
    reps = LANES // group
    cos_t = jnp.tile(cos_g, (1, reps))
    sin_t = jnp.tile(sin_g, (1, reps))
    cos_t = jnp.concatenate([jnp.ones((lc, LANES), F32), cos_t], axis=0)
    sin_t = jnp.concatenate([jnp.zeros((lc, LANES), F32), sin_t], axis=0)
    return cos_t, sin_t


def _na_bias(rpb, rows):
    nq = rows // NA_QROWS
    qi = np.arange(Q_TILE)
    ki = np.arange(NA_KEYS)
    qr_rel, qc = qi // GRID_W, qi % GRID_W
    kr_rel, kc = ki // GRID_W, ki % GRID_W
    win = np.clip(qc - NA_COLS // 2, 0, GRID_W - NA_COLS)
    tabs = []
    for t in (0, min(1, nq - 1), nq - 1):
        ks = int(np.clip(t * NA_QROWS - NA_ROWS // 2, 0, rows - NA_KROWS))
        qr = t * NA_QROWS + qr_rel
        kr = ks + kr_rel
        r0 = np.clip(qr - NA_ROWS // 2, 0, rows - NA_ROWS)
        ok = ((kr[None, :] >= r0[:, None]) & (kr[None, :] < r0[:, None] + NA_ROWS)
              & (kc[None, :] >= win[:, None]) & (kc[None, :] < win[:, None] + NA_COLS))
        dr = np.clip(kr[None, :] - qr[:, None] + NA_ROWS - 1, 0, 2 * NA_ROWS - 2)
        dc = np.clip(kc[None, :] - qc[:, None] + NA_COLS - 1, 0, 2 * NA_COLS - 2)
        tabs.append(jnp.where(jnp.asarray(ok)[None], rpb[:, dr, dc].astype(F32), NEG_INF))
    tabs.append(jnp.full_like(tabs[0], NEG_INF))
    return jnp.stack(tabs)


def _layer_weights(w_in, b_w_q_up, b_w_kv_up, w_branch):
    cuts = np.cumsum([512, 128, 128, 512, 256, 32, 512, 512, 512, 512, 512, 512, 512, 512])
    (a_q, a_k, a_v, b_cq, b_ckv, b_kr, c_q, c_i, c_ff, c_fb, c_g, d_q, d_k, d_v, gates) = jnp.split(w_in, cuts, axis=1)
    d = w_in.shape[0]
    aq = a_q.reshape(d, A_HEADS, HEAD_DIM)
    aq = jnp.stack([aq[:, 0:4], aq[:, 4:8]], axis=2).reshape(d, A_HEADS * HEAD_DIM)
    w = jnp.concatenate([aq, a_k, a_v, b_ckv, b_cq, c_q, c_i, c_ff, c_fb, c_g, d_q, d_k, d_v, gates], axis=1)
    wkr = jnp.zeros((d, LANES), F32).at[:, B_NOPE:B_NOPE + B_ROPE].set(b_kr)
    wq = b_w_q_up.reshape(B_Q_RANK, B_HEADS, B_NOPE + B_ROPE)
    wq = jnp.pad(wq, ((0, 0), (0, 0), (0, LANES - B_NOPE - B_ROPE))).reshape(B_Q_RANK, B_HEADS * LANES)
    wkv = b_w_kv_up.reshape(B_KV_RANK, B_HEADS, B_NOPE + B_V)
    wk = jnp.pad(wkv[:, :, :B_NOPE], ((0, 0), (0, 0), (0, LANES - B_NOPE))).reshape(B_KV_RANK, B_HEADS * LANES)
    wv = wkv[:, :, B_NOPE:].reshape(B_KV_RANK, B_HEADS * B_V)
    wb0 = w_branch[0].reshape(A_HEADS, HEAD_DIM, -1)
    wb0 = jnp.stack([wb0[0:4], wb0[4:8]], axis=1).reshape(A_HEADS * HEAD_DIM, -1)
    wb = jnp.concatenate([wb0[None], w_branch[1:]], axis=0)
    cast = lambda t: t.astype(BF16)
    return cast(w), cast(wkr), cast(wq), cast(wk), cast(wv), cast(wb)


def _group_matrix(bounds):
    gid = np.zeros((LANES,), np.int32)
    size = np.zeros((LANES,), np.float32)
    for g, (lo, hi) in enumerate(bounds):
        gid[lo:hi] = g
        size[lo:hi] = hi - lo
    gmat = (gid[:, None] == gid[None, :]).astype(np.float32)
    return jnp.asarray(gmat, dtype=BF16), jnp.asarray(1.0 / size, dtype=F32).reshape(1, LANES)


def _pad_lanes(v, off=0):
    return jnp.zeros((1, LANES), F32).at[0, off:off + v.shape[0]].set(v.astype(F32))


def kernel(x, c, ctx, c_ctx, w_mod, b_mod, g_norm_mix, g_norm_ffn, w_in, a_gq, a_gk, a_sink, b_g_qa, b_g_kva,
           b_w_q_up, b_w_kv_up, b_gq, b_gk, c_lb_logits, c_g_out, d_gq, d_gk, d_rpb, w_branch, w_out,
           router_w, router_b, w_exp_gate, w_exp_up, w_exp_down):
    nb, l, d = x.shape
    lc = ctx.shape[1]
    s = lc + l
    ntok = nb * s
    depth = w_in.shape[0]
    assert l % Q_TILE == 0 and lc % Q_TILE == 0 and (l // GRID_W) >= NA_KROWS and s % 64 == 0

    xs = jnp.concatenate([ctx, x], axis=1).reshape(ntok, d)

    mod_rows = 16
    cvecs = jnp.zeros((mod_rows, d), F32).at[:nb].set(c).at[nb].set(c_ctx)
    mods = _adaln(cvecs, w_mod, b_mod).reshape(depth, mod_rows, 6, d)

    cos_a, sin_a = _rope_tables(l, lc, HEAD_DIM, 0, HEAD_DIM)
    cos_b, sin_b = _rope_tables(l, lc, B_ROPE, B_NOPE, LANES)
    gmat64, _ = _group_matrix([(0, 64), (64, 128)])
    gmat_b, inv_b = _group_matrix([(0, B_NOPE), (B_NOPE, B_NOPE + B_ROPE), (B_NOPE + B_ROPE, LANES)])
    lb_p = jax.nn.softmax(c_lb_logits.astype(F32), axis=1)
    lb_all = jnp.clip(jnp.cumsum(lb_p, axis=1) - lb_p[:, :1], 0.0, 1.0)
    rw_t = router_w.astype(F32).T
    rb = router_b.astype(F32).reshape(N_EXPERTS, 1)

    for lyr in range(depth):
        w, wkr, wq, wk, wv, wb = _layer_weights(w_in[lyr], b_w_q_up[lyr], b_w_kv_up[lyr], w_branch[lyr])
        m = mods[lyr]
        proj, kr = _in_proj(xs, m, g_norm_mix[lyr], w, wkr, nb, s, lc)

        two = lambda g: jnp.concatenate([g, g]).astype(F32)
        gvecs = jnp.stack([two(a_gq[lyr]), two(a_gk[lyr]), two(d_gq[lyr]), two(d_gk[lyr])])
        qk_ad = _prep_ad(proj, cos_a, sin_a, gvecs, gmat64, s)
        qb, kb, vb = _prep_b(proj, kr, wq, wk, wv, b_g_qa[lyr].reshape(1, -1), b_g_kva[lyr].reshape(1, -1),
                             _pad_lanes(b_gq[lyr]), _pad_lanes(b_gk[lyr]), gmat_b, inv_b, cos_b, sin_b, s)

        oa = _attn_a(a_sink[lyr].astype(F32), qk_ad, proj, nb, s, lc)
        ob = _attn_b(qb, kb, vb, nb, s)
        ofw, obw = _gla(proj, lb_all[:, lyr], nb, s, lc)
        od = _attn_d(qk_ad, proj, _na_bias(d_rpb[lyr], l // GRID_W), nb, s, lc)

        xn, h2c = _merge(oa, ob, ofw, obw, od, proj, xs, m, wb, w_out[lyr].astype(BF16),
                         c_g_out[lyr].reshape(1, -1).astype(F32), g_norm_ffn[lyr].reshape(1, -1), nb, s, lc)

        idx, wts = _router(h2c, rw_t, rb)
        texp, tval, entries, row_w = _route_plan(idx, wts, ntok)
        ybuf = _experts(h2c, texp, tval, entries, row_w, w_exp_gate[lyr].astype(BF16),
                        w_exp_up[lyr].astype(BF16), w_exp_down[lyr].astype(BF16), ntok)
        xs = _combine(xn, ybuf, m, nb, s, lc)

    return xs.reshape(nb, s, d)[:, lc:]
```

```python
import functools

import numpy as np
import jax
import jax.numpy as jnp
from jax import lax
from jax.experimental import pallas as pl
from jax.experimental.pallas import tpu as pltpu

F32 = jnp.float32
BF16 = jnp.bfloat16
HIGHEST = lax.Precision.HIGHEST

GRID_W = 64
HEAD_DIM = 64
ROPE_BASE = 10000.0
EPS = 1e-6
F_MIN = 1e-6
NEG_INF = -1e30
A_HEADS, A_KV_HEADS, A_WINDOW = 8, 2, 128
B_HEADS, B_NOPE, B_ROPE, B_V, B_Q_RANK, B_KV_RANK = 8, 64, 32, 64, 512, 256
C_HEADS, C_DK, C_DV = 4, 128, 128
D_HEADS, NA_ROWS, NA_COLS = 8, 8, 16
N_BRANCH, BRANCH_W = 4, 512
N_EXPERTS, N_GROUPS, TOP_K, D_EXPERT = 16, 4, 2, 1024

LANES = 128
VMEM_LIMIT = 56 * 1024 * 1024

COL_AQ, COL_AK, COL_AV, COL_BCKV, COL_BCQ = 0, 512, 640, 768, 1024
COL_CQ, COL_CI, COL_CFF, COL_CFB, COL_CG = 1536, 2048, 2560, 3072, 3584
COL_DQ, COL_DK, COL_DV, COL_GATES = 4096, 4608, 5120, 5632
PROJ_W = COL_GATES + N_BRANCH * 2048

Q_TILE = 256
GLA_CHUNK = 128
CHUNK_ROWS = 16
EXPERT_TILE = 256


def _cparams(sem, vmem=VMEM_LIMIT):
    return pltpu.CompilerParams(dimension_semantics=sem, vmem_limit_bytes=vmem)


def _sigmoid(x):
    return 1.0 / (1.0 + jnp.exp(-x))


def _silu(x):
    return x * _sigmoid(x)


def _rot_pairs(x, half):
    n = x.shape[-1]
    ax = x.ndim - 1
    lane = lax.broadcasted_iota(jnp.int32, x.shape, ax)
    fwd = pltpu.roll(x, n - half, axis=ax)
    bwd = pltpu.roll(x, half, axis=ax)
    return jnp.where((lane & (2 * half - 1)) < half, fwd, bwd)


def _row_mod(i, mb_ref, mc_ref, idx, tile, tiles_per_batch, lc):
    row = (i % tiles_per_batch) * tile + lax.broadcasted_iota(jnp.int32, (tile, 1), 0)
    return jnp.where(row < lc, mc_ref[0, idx:idx + 1, :], mb_ref[0, idx:idx + 1, :])


def _mod_norm(x, g, shift, scale):
    ms = jnp.mean(x * x, axis=-1, keepdims=True)
    return x * lax.rsqrt(ms + EPS) * g * (1.0 + scale) + shift


def _adaln_kernel(c_ref, w_ref, b_ref, o_ref):
    cv = c_ref[...]
    o_ref[0] = jnp.dot(_silu(cv), w_ref[0], preferred_element_type=F32, precision=HIGHEST) + b_ref[0]


def _adaln(cvecs, w_mod, b_mod):
    depth, d, n6 = w_mod.shape
    rows = cvecs.shape[0]
    tn = 1024
    return pl.pallas_call(
        _adaln_kernel,
        grid=(depth, n6 // tn),
        in_specs=[pl.BlockSpec((rows, d), lambda l, j: (0, 0)),
                  pl.BlockSpec((1, d, tn), lambda l, j: (l, 0, j)),
                  pl.BlockSpec((1, 1, tn), lambda l, j: (l, 0, j))],
        out_specs=pl.BlockSpec((1, rows, tn), lambda l, j: (l, 0, j)),
        out_shape=jax.ShapeDtypeStruct((depth, rows, n6), F32),
        compiler_params=_cparams(("parallel", "parallel")),
        name="adaln",
    )(cvecs, w_mod, b_mod.reshape(depth, 1, n6))


def _inproj_kernel(x_ref, mb_ref, mc_ref, g_ref, w_ref, wkr_ref, o_ref, kr_ref, h_scr, *, tile, tpb, lc):
    i = pl.program_id(0)

    @pl.when(pl.program_id(1) == 0)
    def _():
        shift = _row_mod(i, mb_ref, mc_ref, 0, tile, tpb, lc)
        scale = _row_mod(i, mb_ref, mc_ref, 1, tile, tpb, lc)
        hb = _mod_norm(x_ref[...], g_ref[...], shift, scale).astype(BF16)
        h_scr[...] = hb
        kr_ref[...] = jnp.dot(hb, wkr_ref[...], preferred_element_type=F32).astype(BF16)

    o_ref[...] = jnp.dot(h_scr[...], w_ref[...], preferred_element_type=F32).astype(BF16)


def _in_proj(xs, mods, g, w, wkr, nb, s, lc):
    ntok, d = xs.shape
    tile = s // 4
    tpb = s // tile
    tn = 512
    kern = functools.partial(_inproj_kernel, tile=tile, tpb=tpb, lc=lc)
    return pl.pallas_call(
        kern,
        grid=(ntok // tile, PROJ_W // tn),
        in_specs=[pl.BlockSpec((tile, d), lambda i, j: (i, 0)),
                  pl.BlockSpec((1, 6, d), lambda i, j: (i // tpb, 0, 0)),
                  pl.BlockSpec((1, 6, d), lambda i, j: (nb, 0, 0)),
                  pl.BlockSpec((1, d), lambda i, j: (0, 0)),
                  pl.BlockSpec((d, tn), lambda i, j: (0, j)),
                  pl.BlockSpec((d, LANES), lambda i, j: (0, 0))],
        out_specs=[pl.BlockSpec((tile, tn), lambda i, j: (i, j)),
                   pl.BlockSpec((tile, LANES), lambda i, j: (i, 0))],
        out_shape=[jax.ShapeDtypeStruct((ntok, PROJ_W), BF16),
                   jax.ShapeDtypeStruct((ntok, LANES), BF16)],
        scratch_shapes=[pltpu.VMEM((tile, d), BF16)],
        compiler_params=_cparams(("parallel", "arbitrary")),
        name="in_proj",
    )(xs, mods, mods, g.reshape(1, d), w, wkr)


def _group_norm(x, gmat, inv_size, gvec):
    ss = jnp.dot((x * x).astype(BF16), gmat, preferred_element_type=F32)
    return x * lax.rsqrt(ss * inv_size + EPS) * gvec


def _prep_ad_kernel(a_ref, d_ref, cos_ref, sin_ref, g_ref, gmat_ref, o_ref):
    gmat = gmat_ref[...]
    cos = cos_ref[...]
    sin = sin_ref[...]
    scale = HEAD_DIM ** -0.5
    inv = 1.0 / HEAD_DIM
    for blk in range(5):
        x = a_ref[:, blk * LANES:(blk + 1) * LANES].astype(F32)
        gi = 0 if blk < 4 else 1
        xn = _group_norm(x, gmat, inv, g_ref[gi:gi + 1, :])
        xr = xn * cos + _rot_pairs(xn, HEAD_DIM // 2) * sin
        if blk < 4:
            xr = xr * scale
        ob = blk if blk < 4 else 12
        o_ref[:, ob * LANES:(ob + 1) * LANES] = xr.astype(BF16)
    for blk in range(8):
        x = d_ref[:, blk * LANES:(blk + 1) * LANES].astype(F32)
        gi = 2 if blk < 4 else 3
        xn = _group_norm(x, gmat, inv, g_ref[gi:gi + 1, :])
        if blk < 4:
            xn = xn * scale
        o_ref[:, (4 + blk) * LANES:(5 + blk) * LANES] = xn.astype(BF16)


def _prep_ad(proj, cos, sin, gvecs, gmat, s):
    ntok = proj.shape[0]
    tile = s // 4
    tpb = s // tile
    wa, wd = 5 * LANES, 8 * LANES
    return pl.pallas_call(
        _prep_ad_kernel,
        grid=(ntok // tile,),
        in_specs=[pl.BlockSpec((tile, wa), lambda i: (i, COL_AQ // wa)),
                  pl.BlockSpec((tile, wd), lambda i: (i, COL_DQ // wd)),
                  pl.BlockSpec((tile, LANES), lambda i: (i % tpb, 0)),
                  pl.BlockSpec((tile, LANES), lambda i: (i % tpb, 0)),
                  pl.BlockSpec((4, LANES), lambda i: (0, 0)),
                  pl.BlockSpec((LANES, LANES), lambda i: (0, 0))],
        out_specs=pl.BlockSpec((tile, 13 * LANES), lambda i: (i, 0)),
        out_shape=jax.ShapeDtypeStruct((ntok, 13 * LANES), BF16),
        compiler_params=_cparams(("parallel",)),
        name="prep_ad",
    )(proj, proj, cos, sin, gvecs, gmat)


def _prep_b_kernel(cq_ref, ckv_ref, kr_ref, wq_ref, wk_ref, wv_ref, gqa_ref, gkva_ref, gq_ref, gk_ref,
                   gmat_ref, inv_ref, cos_ref, sin_ref, q_ref, k_ref, v_ref):
    gmat = gmat_ref[...]
    inv = inv_ref[...]
    cos = cos_ref[...]
    sin = sin_ref[...]
    scale = (B_NOPE + B_ROPE) ** -0.5

    def rms(x, g):
        return x * lax.rsqrt(jnp.mean(x * x, axis=-1, keepdims=True) + EPS) * g

    cq = rms(cq_ref[...].astype(F32), gqa_ref[...]).astype(BF16)
    ckv = rms(ckv_ref[...].astype(F32), gkva_ref[...]).astype(BF16)
    q = jnp.dot(cq, wq_ref[...], preferred_element_type=F32)
    kn = jnp.dot(ckv, wk_ref[...], preferred_element_type=F32)
    v_ref[...] = jnp.dot(ckv, wv_ref[...], preferred_element_type=F32).astype(BF16)
    kro = _group_norm(kr_ref[...].astype(F32), gmat, inv, gk_ref[...])
    kro = kro * cos + _rot_pairs(kro, B_ROPE // 2) * sin
    for h in range(B_HEADS):
        sl = slice(h * LANES, (h + 1) * LANES)
        qh = _group_norm(q[:, sl], gmat, inv, gq_ref[...])
        qh = (qh * cos + _rot_pairs(qh, B_ROPE // 2) * sin) * scale
        q_ref[:, sl] = qh.astype(BF16)
        kh = _group_norm(kn[:, sl], gmat, inv, gk_ref[...]) + kro
        k_ref[:, sl] = kh.astype(BF16)


def _prep_b(proj, kr, wq, wk, wv, gqa, gkva, gq, gk, gmat, inv, cos, sin, s):
    ntok = proj.shape[0]
    tile = s // 4
    tpb = s // tile
    hw = B_HEADS * LANES
    full = lambda shape: pl.BlockSpec(shape, lambda i: (0, 0))
    return pl.pallas_call(
        _prep_b_kernel,
        grid=(ntok // tile,),
        in_specs=[pl.BlockSpec((tile, B_Q_RANK), lambda i: (i, COL_BCQ // B_Q_RANK)),
                  pl.BlockSpec((tile, B_KV_RANK), lambda i: (i, COL_BCKV // B_KV_RANK)),
                  pl.BlockSpec((tile, LANES), lambda i: (i, 0)),
                  full((B_Q_RANK, hw)), full((B_KV_RANK, hw)), full((B_KV_RANK, B_HEADS * B_V)),
                  full((1, B_Q_RANK)), full((1, B_KV_RANK)), full((1, LANES)), full((1, LANES)),
                  full((LANES, LANES)), full((1, LANES)),
                  pl.BlockSpec((tile, LANES), lambda i: (i % tpb, 0)),
                  pl.BlockSpec((tile, LANES), lambda i: (i % tpb, 0))],
        out_specs=[pl.BlockSpec((tile, hw), lambda i: (i, 0)),
                   pl.BlockSpec((tile, hw), lambda i: (i, 0)),
                   pl.BlockSpec((tile, B_HEADS * B_V), lambda i: (i, 0))],
        out_shape=[jax.ShapeDtypeStruct((ntok, hw), BF16),
                   jax.ShapeDtypeStruct((ntok, hw), BF16),
                   jax.ShapeDtypeStruct((ntok, B_HEADS * B_V), BF16)],
        compiler_params=_cparams(("parallel",)),
        name="prep_b",
    )(proj, proj, kr, wq, wk, wv, gqa, gkva, gq, gk, gmat, inv, cos, sin)


def _nt_dot(a, b):
    return lax.dot_general(a, b, (((1,), (1,)), ((), ())), preferred_element_type=F32)


def _half_mask(shape, half):
    lane = lax.broadcasted_iota(jnp.int32, shape, len(shape) - 1)
    return (lane >= HEAD_DIM) if half else (lane < HEAD_DIM)


def _joint_attend(s, sc, vw, vc, sink=None):
    m = jnp.maximum(jnp.max(s, axis=-1, keepdims=True), jnp.max(sc, axis=-1, keepdims=True))
    if sink is not None:
        m = jnp.maximum(m, sink)
    p = jnp.exp(s - m)
    pc = jnp.exp(sc - m)
    den = jnp.sum(p, axis=-1, keepdims=True) + jnp.sum(pc, axis=-1, keepdims=True)
    if sink is not None:
        den = den + jnp.exp(sink - m)
    o = (jnp.dot(p.astype(BF16), vw, preferred_element_type=F32)
         + jnp.dot(pc.astype(BF16), vc, preferred_element_type=F32))
    return o / den


def _attn_a_kernel(sink_ref, q_ref, k_ref, v_ref, o_ref, *, lc, l):
    qt = pl.program_id(1)
    span = Q_TILE + 2 * A_WINDOW
    qstart = (qt - 1) * Q_TILE
    ws = jnp.clip(qstart - A_WINDOW, 0, l - span)
    wrow = pl.multiple_of(lc + ws, LANES)
    kw = k_ref[pl.ds(wrow, span), :]
    vw = v_ref[pl.ds(wrow, span), :]
    kc = k_ref[0:lc, :]
    vc = v_ref[0:lc, :]
    kpos = ws + lax.broadcasted_iota(jnp.int32, (Q_TILE, span), 1)
    qoff = jnp.where(qt > 0, qstart, -(1 << 20))
    qpos = qoff + lax.broadcasted_iota(jnp.int32, (Q_TILE, span), 0)
    ok = jnp.abs(kpos - qpos) <= A_WINDOW
    for blk in range(4):
        qb = q_ref[:, blk * LANES:(blk + 1) * LANES]
        halves = []
        for half in range(2):
            head = blk + 4 * half
            qm = jnp.where(_half_mask(qb.shape, half), qb, jnp.zeros_like(qb))
            s = jnp.where(ok, _nt_dot(qm, kw), NEG_INF)
            sc = _nt_dot(qm, kc)
            halves.append(_joint_attend(s, sc, vw, vc, sink_ref[head]))
        out = jnp.where(_half_mask(halves[0].shape, 0), halves[0], halves[1])
        o_ref[:, blk * LANES:(blk + 1) * LANES] = out.astype(BF16)


def _attn_a(sink, qk, proj, nb, s, lc):
    ntok = qk.shape[0]
    nq = s // Q_TILE
    kern = functools.partial(_attn_a_kernel, lc=lc, l=s - lc)
    return pl.pallas_call(
        kern,
        grid_spec=pltpu.PrefetchScalarGridSpec(
            num_scalar_prefetch=0,
            grid=(nb, nq),
            in_specs=[pl.BlockSpec(memory_space=pltpu.SMEM),
                      pl.BlockSpec((Q_TILE, 4 * LANES), lambda b, t: (b * nq + t, 0)),
                      pl.BlockSpec((s, LANES), lambda b, t: (b, 12)),
                      pl.BlockSpec((s, LANES), lambda b, t: (b, COL_AV // LANES))],
            out_specs=pl.BlockSpec((Q_TILE, 4 * LANES), lambda b, t: (b * nq + t, 0)),
        ),
        out_shape=jax.ShapeDtypeStruct((ntok, 4 * LANES), BF16),
        compiler_params=_cparams(("parallel", "arbitrary")),
        name="attn_a",
    )(sink, qk, qk, proj)


def _attn_b_kernel(q_ref, k_ref, v_ref, o_ref, s_scr, p_scr, *, s, lc):
    qt = pl.program_id(2)

    def head(hh, nkeys):
        hl = slice(hh * LANES, (hh + 1) * LANES)
        q = q_ref[:, hl]
        for c in range(nkeys // Q_TILE):
            rows = slice(c * Q_TILE, (c + 1) * Q_TILE)
            s_scr[hh, :, rows] = _nt_dot(q, k_ref[rows, hl])
        mx = s_scr[hh, :, 0:LANES]
        for j in range(1, nkeys // LANES):
            mx = jnp.maximum(mx, s_scr[hh, :, j * LANES:(j + 1) * LANES])
        m = jnp.broadcast_to(jnp.max(mx, axis=-1, keepdims=True), (Q_TILE, LANES))
        lsum = jnp.zeros((Q_TILE, LANES), F32)
        for j in range(nkeys // LANES):
            cols = slice(j * LANES, (j + 1) * LANES)
            p = jnp.exp(s_scr[hh, :, cols] - m)
            lsum = lsum + p
            p_scr[hh, :, cols] = p.astype(BF16)
        den = jnp.sum(lsum, axis=-1, keepdims=True)
        o = jnp.dot(p_scr[hh, :, 0:nkeys], v_ref[0:nkeys, :], preferred_element_type=F32)
        return o / den

    def both(nkeys):
        r0 = head(0, nkeys)
        r1 = head(1, nkeys)
        o_ref[...] = jnp.where(_half_mask(r0.shape, 0), r0, r1).astype(BF16)

    @pl.when(qt == 0)
    def _():
        both(lc)

    @pl.when(qt > 0)
    def _():
        both(s)


def _attn_b(qb, kb, vb, nb, s, lc):
    ntok = qb.shape[0]
    nq = s // Q_TILE
    kern = functools.partial(_attn_b_kernel, s=s, lc=lc)
    return pl.pallas_call(
        kern,
        grid=(nb, B_HEADS // 2, nq),
        in_specs=[pl.BlockSpec((Q_TILE, 2 * LANES), lambda b, p, t: (b * nq + t, p)),
                  pl.BlockSpec((s, 2 * LANES), lambda b, p, t: (b, p)),
                  pl.BlockSpec((s, LANES), lambda b, p, t: (b, p))],
        out_specs=pl.BlockSpec((Q_TILE, LANES), lambda b, p, t: (b * nq + t, p)),
        out_shape=jax.ShapeDtypeStruct((ntok, B_HEADS * B_V), BF16),
        scratch_shapes=[pltpu.VMEM((2, Q_TILE, s), F32), pltpu.VMEM((2, Q_TILE, s), BF16)],
        compiler_params=_cparams(("parallel", "parallel", "arbitrary")),
        name="attn_b",
    )(qb, kb, vb)


NA_QROWS = Q_TILE // GRID_W
NA_KROWS = NA_QROWS + NA_ROWS
NA_KEYS = NA_KROWS * GRID_W


def _attn_d_kernel(q_ref, k_ref, v_ref, bias_ref, o_ref, *, lc, rows):
    qt = pl.program_id(1)
    ks = jnp.clip((qt - 1) * NA_QROWS - NA_ROWS // 2, 0, rows - NA_KROWS)
    wrow = pl.multiple_of(lc + ks * GRID_W, Q_TILE)
    for blk in range(4):
        sl = slice(blk * LANES, (blk + 1) * LANES)
        qb = q_ref[:, sl]
        kw = k_ref[pl.ds(wrow, NA_KEYS), sl]
        vw = v_ref[pl.ds(wrow, NA_KEYS), sl]
        kc = k_ref[0:lc, sl]
        vc = v_ref[0:lc, sl]
        halves = []
        for half in range(2):
            head = 2 * blk + half
            qm = jnp.where(_half_mask(qb.shape, half), qb, jnp.zeros_like(qb))
            s = _nt_dot(qm, kw) + bias_ref[0, head]
            sc = _nt_dot(qm, kc)
            halves.append(_joint_attend(s, sc, vw, vc))
        out = jnp.where(_half_mask(halves[0].shape, 0), halves[0], halves[1])
        o_ref[:, sl] = out.astype(BF16)


def _na_variant(t, nq):
    return jnp.where(t == 0, 3, jnp.where(t == 1, 0, jnp.where(t == nq - 1, 2, 1)))


def _attn_d(qk, proj, bias, nb, s, lc):
    ntok = qk.shape[0]
    nq = s // Q_TILE
    rows = (s - lc) // GRID_W
    kern = functools.partial(_attn_d_kernel, lc=lc, rows=rows)
    w = 4 * LANES
    return pl.pallas_call(
        kern,
        grid=(nb, nq),
        in_specs=[pl.BlockSpec((Q_TILE, w), lambda b, t: (b * nq + t, 1)),
                  pl.BlockSpec((s, w), lambda b, t: (b, 2)),
                  pl.BlockSpec((s, w), lambda b, t: (b, COL_DV // w)),
                  pl.BlockSpec((1, D_HEADS, Q_TILE, NA_KEYS), lambda b, t: (_na_variant(t, nq), 0, 0, 0))],
        out_specs=pl.BlockSpec((Q_TILE, w), lambda b, t: (b * nq + t, 0)),
        out_shape=jax.ShapeDtypeStruct((ntok, w), BF16),
        compiler_params=_cparams(("parallel", "arbitrary")),
        name="attn_d",
    )(qk, qk, proj, bias)


def _cumsum_rows(x, reverse):
    n = x.shape[0]
    row = lax.broadcasted_iota(jnp.int32, x.shape, 0)
    step = 1
    while step < n:
        if reverse:
            x = x + jnp.where(row < n - step, pltpu.roll(x, n - step, axis=0), 0.0)
        else:
            x = x + jnp.where(row >= step, pltpu.roll(x, step, axis=0), 0.0)
        step *= 2
    return x


def _block_ref(b, size, idx):
    n, w = b.shape
    b3 = b.reshape(n // size, size, w)
    return jnp.broadcast_to(b3[:, idx:idx + 1, :], b3.shape).reshape(n, w)


def _gla_chunk(q, v, fpre, lb, st_ref, reverse):
    c = q.shape[0]
    fg = lb + (1.0 - lb) * _sigmoid(fpre)
    logf = jnp.log(jnp.maximum(fg, F_MIN))
    k = 1.0 - fg
    qs = _silu(q)
    b = _cumsum_rows(logf, reverse)
    row = lax.broadcasted_iota(jnp.int32, (c, c), 0)
    col = lax.broadcasted_iota(jnp.int32, (c, c), 1)
    xor = row ^ col
    after = (row < col) if reverse else (row > col)

    base = 8
    bref = _block_ref(b, base, base // 2 if reverse else base // 2 - 1)
    qe = (qs * jnp.exp(b - bref)).astype(BF16)
    ke = (k * jnp.exp(bref - b)).astype(BF16)
    a = jnp.where((xor < base) & (after | (row == col)), _nt_dot(qe, ke), 0.0)
    m = base
    while m < c:
        bref = _block_ref(b, 2 * m, m if reverse else m - 1)
        qe = (qs * jnp.exp(jnp.minimum(b - bref, 0.0))).astype(BF16)
        ke = (k * jnp.exp(jnp.minimum(bref - b, 0.0))).astype(BF16)
        a = jnp.where((xor >= m) & (xor < 2 * m) & after, _nt_dot(qe, ke), a)
        m *= 2

    st = st_ref[...]
    vb = v.astype(BF16)
    o = (jnp.dot(a.astype(BF16), vb, preferred_element_type=F32)
         + _nt_dot((qs * jnp.exp(b)).astype(BF16), st.astype(BF16)))
    btot = b[0:1, :] if reverse else b[c - 1:c, :]
    kdec = (k * jnp.exp(btot - b)).astype(BF16)
    upd = lax.dot_general(vb, kdec, (((0,), (0,)), ((), ())), preferred_element_type=F32)
    st_ref[...] = st * jnp.exp(btot) + upd
    return o


def _gla_kernel(qf_ref, vf_ref, ff_ref, qb_ref, vb_ref, fb_ref, lb_ref, of_ref, ob_ref, stf, stb):
    @pl.when(pl.program_id(2) == 0)
    def _():
        stf[...] = jnp.zeros_like(stf)
        stb[...] = jnp.zeros_like(stb)

    of_ref[...] = _gla_chunk(qf_ref[...].astype(F32), vf_ref[...].astype(F32), ff_ref[...].astype(F32),
                             lb_ref[0:1, :], stf, False)
    ob_ref[...] = _gla_chunk(qb_ref[...].astype(F32), vb_ref[...].astype(F32), fb_ref[...].astype(F32),
                             lb_ref[1:2, :], stb, True)


def _gla(proj, lb, nb, s, lc):
    ntok = proj.shape[0]
    c = GLA_CHUNK
    ns = s // c
    nctx = lc // c

    def bw_chunk(t):
        return jnp.where(t < nctx, nctx - 1 - t, ns - 1 - (t - nctx))

    def fw(col):
        return pl.BlockSpec((c, LANES), lambda b, h, t: (b * ns + t, col // LANES + h))

    def bw(col):
        return pl.BlockSpec((c, LANES), lambda b, h, t: (b * ns + bw_chunk(t), col // LANES + h))

    return pl.pallas_call(
        _gla_kernel,
        grid=(nb, C_HEADS, ns),
        in_specs=[fw(COL_CQ), fw(COL_CI), fw(COL_CFF), bw(COL_CQ), bw(COL_CI), bw(COL_CFB),
                  pl.BlockSpec((2, LANES), lambda b, h, t: (0, h))],
        out_specs=[pl.BlockSpec((c, LANES), lambda b, h, t: (b * ns + t, h)),
                   pl.BlockSpec((c, LANES), lambda b, h, t: (b * ns + bw_chunk(t), h))],
        out_shape=[jax.ShapeDtypeStruct((ntok, C_HEADS * C_DV), F32)] * 2,
        scratch_shapes=[pltpu.VMEM((C_DV, C_DK), F32), pltpu.VMEM((C_DV, C_DK), F32)],
        compiler_params=_cparams(("parallel", "parallel", "arbitrary")),
        name="gla",
    )(proj, proj, proj, proj, proj, proj, lb)


def _mix_kernel(oa_ref, ob_ref, ofw_ref, obw_ref, cg_ref, od_ref, g0_ref, g1_ref, g2_ref, g3_ref,
                wb_ref, gout_ref, y_ref, oc_scr):
    @pl.when(pl.program_id(1) == 0)
    def _():
        gout = gout_ref[...]
        for h in range(C_HEADS):
            sl = slice(h * LANES, (h + 1) * LANES)
            o = ofw_ref[:, sl] + obw_ref[:, sl]
            on = o * lax.rsqrt(jnp.mean(o * o, axis=-1, keepdims=True) + EPS) * gout
            oc_scr[:, sl] = (on * _silu(cg_ref[:, sl].astype(F32))).astype(BF16)

    outs = (oa_ref[...], ob_ref[...], oc_scr[...], od_ref[...])
    gates = (g0_ref, g1_ref, g2_ref, g3_ref)
    y = None
    for br in range(N_BRANCH):
        t = _sigmoid(gates[br][...].astype(F32)) * jnp.dot(outs[br], wb_ref[br], preferred_element_type=F32)
        y = t if y is None else y + t
    y_ref[...] = y.astype(BF16)


def _branch_mix(oa, ob, ofw, obw, od, proj, wb, gout, s):
    ntok = oa.shape[0]
    d = wb.shape[2]
    tile = s // 8
    tn = 512
    row512 = pl.BlockSpec((tile, 512), lambda i, n: (i, 0))

    def gate(br):
        return pl.BlockSpec((tile, tn), lambda i, n: (i, (COL_GATES + br * d) // tn + n))

    return pl.pallas_call(
        _mix_kernel,
        grid=(ntok // tile, d // tn),
        in_specs=[row512, row512, row512, row512,
                  pl.BlockSpec((tile, 512), lambda i, n: (i, COL_CG // 512)),
                  row512, gate(0), gate(1), gate(2), gate(3),
                  pl.BlockSpec((N_BRANCH, BRANCH_W, tn), lambda i, n: (0, 0, n)),
                  pl.BlockSpec((1, LANES), lambda i, n: (0, 0))],
        out_specs=pl.BlockSpec((tile, tn), lambda i, n: (i, n)),
        out_shape=jax.ShapeDtypeStruct((ntok, d), BF16),
        scratch_shapes=[pltpu.VMEM((tile, 512), BF16)],
        compiler_params=_cparams(("parallel", "arbitrary")),
        name="branch_mix",
    )(oa, ob, ofw, obw, proj, od, proj, proj, proj, proj, wb, gout)


def _outproj_kernel(y_ref, x_ref, mb_ref, mc_ref, wo_ref, gffn_ref, xn_ref, h2_ref, *, tile, tpb, lc):
    i = pl.program_id(0)
    out = jnp.dot(y_ref[...], wo_ref[...], preferred_element_type=F32)
    xn = x_ref[...] + _row_mod(i, mb_ref, mc_ref, 2, tile, tpb, lc) * out
    xn_ref[...] = xn
    h2 = _mod_norm(xn, gffn_ref[...], _row_mod(i, mb_ref, mc_ref, 3, tile, tpb, lc),
                   _row_mod(i, mb_ref, mc_ref, 4, tile, tpb, lc))
    for j in range(CHUNK_ROWS):
        h2_ref[pl.ds(j, tile, stride=CHUNK_ROWS), :] = h2[:, j * LANES:(j + 1) * LANES]


def _out_proj(y, xs, mods, wo, gffn, nb, s, lc):
    ntok, d = xs.shape
    tile = s // 8
    tpb = s // tile
    kern = functools.partial(_outproj_kernel, tile=tile, tpb=tpb, lc=lc)
    return pl.pallas_call(
        kern,
        grid=(ntok // tile,),
        in_specs=[pl.BlockSpec((tile, d), lambda i: (i, 0)),
                  pl.BlockSpec((tile, d), lambda i: (i, 0)),
                  pl.BlockSpec((1, 6, d), lambda i: (i // tpb, 0, 0)),
                  pl.BlockSpec((1, 6, d), lambda i: (nb, 0, 0)),
                  pl.BlockSpec((d, d), lambda i: (0, 0), pipeline_mode=pl.Buffered(1)),
                  pl.BlockSpec((1, d), lambda i: (0, 0))],
        out_specs=[pl.BlockSpec((tile, d), lambda i: (i, 0)),
                   pl.BlockSpec((tile * CHUNK_ROWS, LANES), lambda i: (i, 0))],
        out_shape=[jax.ShapeDtypeStruct((ntok, d), F32),
                   jax.ShapeDtypeStruct((ntok * CHUNK_ROWS, LANES), F32)],
        compiler_params=_cparams(("parallel",)),
        name="out_proj",
    )(y, xs, mods, mods, wo, gffn)


def _router_kernel(h_ref, rw_ref, rb_ref, idx_ref, w_ref, *, tile):
    logits = None
    for j in range(CHUNK_ROWS):
        hj = h_ref[pl.ds(j, tile, stride=CHUNK_ROWS), :]
        part = lax.dot_general(rw_ref[:, j * LANES:(j + 1) * LANES], hj, (((1,), (1,)), ((), ())),
                               preferred_element_type=F32, precision=HIGHEST)
        logits = part if logits is None else logits + part
    scores = _sigmoid(logits)
    biased = scores + rb_ref[...]
    sc = [scores[e:e + 1, :] for e in range(N_EXPERTS)]
    bi = [biased[e:e + 1, :] for e in range(N_EXPERTS)]
    per = N_EXPERTS // N_GROUPS
    best_g = jnp.zeros((1, tile), jnp.int32)
    best_s = None
    for g in range(N_GROUPS):
        members = bi[g * per:(g + 1) * per]
        gs = None
        for a in range(per):
            for b in range(a + 1, per):
                pair = members[a] + members[b]
                gs = pair if gs is None else jnp.maximum(gs, pair)
        if best_s is None:
            best_s = gs
        else:
            better = gs > best_s
            best_g = jnp.where(better, g, best_g)
            best_s = jnp.where(better, gs, best_s)
    picks = []
    taken = jnp.full((1, tile), -1, jnp.int32)
    for _ in range(TOP_K):
        bv = jnp.full((1, tile), -3e38, F32)
        be = jnp.zeros((1, tile), jnp.int32)
        bw = jnp.zeros((1, tile), F32)
        for e in range(N_EXPERTS):
            val = jnp.where(best_g == e // per, bi[e], NEG_INF)
            better = (val > bv) & (taken != e)
            bv = jnp.where(better, val, bv)
            be = jnp.where(better, e, be)
            bw = jnp.where(better, sc[e], bw)
        picks.append((be, bw))
        taken = be
    tot = picks[0][1] + picks[1][1]
    idx_ref[...] = jnp.concatenate([picks[0][0], picks[1][0]], axis=0)
    w_ref[...] = jnp.concatenate([picks[0][1] / tot, picks[1][1] / tot], axis=0)


def _router(h2c, rw_t, rb):
    ntok = h2c.shape[0] // CHUNK_ROWS
    tile = 1024 if ntok % 1024 == 0 else 256
    d = rw_t.shape[1]
    kern = functools.partial(_router_kernel, tile=tile)
    return pl.pallas_call(
        kern,
        grid=(ntok // tile,),
        in_specs=[pl.BlockSpec((tile * CHUNK_ROWS, LANES), lambda i: (i, 0)),
                  pl.BlockSpec((N_EXPERTS, d), lambda i: (0, 0)),
                  pl.BlockSpec((N_EXPERTS, 1), lambda i: (0, 0))],
        out_specs=[pl.BlockSpec((TOP_K, tile), lambda i: (0, i)),
                   pl.BlockSpec((TOP_K, tile), lambda i: (0, i))],
        out_shape=[jax.ShapeDtypeStruct((TOP_K, ntok), jnp.int32),
                   jax.ShapeDtypeStruct((TOP_K, ntok), F32)],
        compiler_params=_cparams(("parallel",)),
        name="router",
    )(h2c, rw_t, rb)


def _expert_kernel(texp_ref, tval_ref, ent_ref, h_hbm, rw_ref, wg_ref, wu_ref, wd_ref, y_hbm,
                   gbuf, xbuf, obuf, gsem, ssem, *, ntok, tile):
    i = pl.program_id(0)
    n = pl.num_programs(0)
    slot = i % 2
    slab = tile * CHUNK_ROWS

    def rows(r):
        return pl.ds(pl.multiple_of(r * CHUNK_ROWS, CHUNK_ROWS), CHUNK_ROWS)

    def start_gather(t, sl):
        def body(r, carry):
            e = ent_ref[t * tile + r]
            src = jnp.where(e >= 2 * ntok, 0, jnp.where(e >= ntok, e - ntok, e))
            pltpu.make_async_copy(h_hbm.at[rows(src), :], gbuf.at[sl, rows(r), :], gsem.at[sl]).start()
            return carry
        lax.fori_loop(0, tile, body, 0, unroll=8)

    def wait_gather(sl):
        pltpu.make_async_copy(h_hbm.at[pl.ds(0, slab), :], gbuf.at[sl], gsem.at[sl]).wait()

    def start_scatter(t, sl):
        def body(r, carry):
            e = ent_ref[t * tile + r]
            pltpu.make_async_copy(obuf.at[sl, rows(r), :], y_hbm.at[rows(e), :], ssem.at[sl]).start()
            return carry
        lax.fori_loop(0, tile, body, 0, unroll=8)

    def wait_scatter(sl):
        pltpu.make_async_copy(obuf.at[sl], y_hbm.at[pl.ds(0, slab), :], ssem.at[sl]).wait()

    @pl.when(i == 0)
    def _():
        start_gather(0, 0)

    @pl.when(i + 1 < n)
    def _():
        start_gather(i + 1, 1 - slot)

    wait_gather(slot)

    @pl.when(i >= 2)
    def _():
        wait_scatter(slot)

    @pl.when(tval_ref[i] > 0)
    def _():
        for j in range(CHUNK_ROWS):
            xbuf[:, j * LANES:(j + 1) * LANES] = gbuf[slot, pl.ds(j, tile, stride=CHUNK_ROWS), :].astype(BF16)
        x = xbuf[...]
        hg = jnp.dot(x, wg_ref[0], preferred_element_type=F32)
        hu = jnp.dot(x, wu_ref[0], preferred_element_type=F32)
        he = (_silu(hg) * hu).astype(BF16)
        y = jnp.dot(he, wd_ref[0], preferred_element_type=F32) * rw_ref[...]
        for j in range(CHUNK_ROWS):
            obuf[slot, pl.ds(j, tile, stride=CHUNK_ROWS), :] = y[:, j * LANES:(j + 1) * LANES]

    @pl.when(tval_ref[i] == 0)
    def _():
        obuf[slot] = jnp.zeros((slab, LANES), F32)

    start_scatter(i, slot)

    @pl.when(i == n - 1)
    def _():
        @pl.when(n > 1)
        def _():
            wait_scatter(1 - slot)
        wait_scatter(slot)


def _experts(h2c, tile_exp, tile_valid, entries, row_w, wg, wu, wd, ntok):
    tile = EXPERT_TILE
    p = entries.shape[0]
    d = wg.shape[1]
    de = wg.shape[2]
    kern = functools.partial(_expert_kernel, ntok=ntok, tile=tile)
    return pl.pallas_call(
        kern,
        grid_spec=pltpu.PrefetchScalarGridSpec(
            num_scalar_prefetch=3,
            grid=(p // tile,),
            in_specs=[pl.BlockSpec(memory_space=pl.ANY),
                      pl.BlockSpec((tile, 1), lambda i, te, tv, en: (i, 0)),
                      pl.BlockSpec((1, d, de), lambda i, te, tv, en: (te[i], 0, 0)),
                      pl.BlockSpec((1, d, de), lambda i, te, tv, en: (te[i], 0, 0)),
                      pl.BlockSpec((1, de, d), lambda i, te, tv, en: (te[i], 0, 0))],
            out_specs=pl.BlockSpec(memory_space=pl.ANY),
            scratch_shapes=[pltpu.VMEM((2, tile * CHUNK_ROWS, LANES), F32),
                            pltpu.VMEM((tile, d), BF16),
                            pltpu.VMEM((2, tile * CHUNK_ROWS, LANES), F32),
                            pltpu.SemaphoreType.DMA((2,)),
                            pltpu.SemaphoreType.DMA((2,))]),
        out_shape=jax.ShapeDtypeStruct((p * CHUNK_ROWS, LANES), F32),
        compiler_params=_cparams(("arbitrary",)),
        name="experts",
    )(tile_exp, tile_valid, entries, h2c, row_w, wg, wu, wd)


def _route_plan(idx, w, ntok):
    tile = EXPERT_TILE
    n_ent = TOP_K * ntok
    p = n_ent + N_EXPERTS * tile
    flat_e = idx.reshape(-1)
    ent = jnp.arange(n_ent, dtype=jnp.int32)
    order = jnp.sort(flat_e * n_ent + ent) % n_ent
    counts = jnp.sum((flat_e[:, None] == jnp.arange(N_EXPERTS)[None, :]).astype(jnp.int32), axis=0)
    padded = ((counts + tile - 1) // tile) * tile
    gstart = jnp.cumsum(padded) - padded
    gend = gstart + padded
    ustart = jnp.cumsum(counts) - counts
    tstart = jnp.arange(p // tile, dtype=jnp.int32) * tile
    texp = jnp.sum((tstart[:, None] >= gend[None, :]).astype(jnp.int32), axis=1)
    used = tstart < gend[-1]
    texp = jnp.where(used, texp, N_EXPERTS - 1).astype(jnp.int32)
    tval = used.astype(jnp.int32)
    row = jnp.arange(p, dtype=jnp.int32)
    row_e = jnp.repeat(texp, tile)
    local = row - gstart[row_e]
    real = jnp.repeat(used, tile) & (local < counts[row_e])
    src = order[jnp.clip(ustart[row_e] + local, 0, n_ent - 1)]
    pad_id = n_ent + jnp.cumsum(1 - real.astype(jnp.int32)) - 1
    entries = jnp.where(real, src, pad_id).astype(jnp.int32)
    row_w = jnp.where(real, w.reshape(-1)[src], 0.0).reshape(p, 1)
    return texp, tval, entries, row_w


def _combine_kernel(x_ref, y0_ref, y1_ref, mb_ref, mc_ref, o_ref, *, tile, tpb, lc):
    gate = _row_mod(pl.program_id(0), mb_ref, mc_ref, 5, tile, tpb, lc)
    for j in range(CHUNK_ROWS):
        sl = slice(j * LANES, (j + 1) * LANES)
        y = y0_ref[pl.ds(j, tile, stride=CHUNK_ROWS), :] + y1_ref[pl.ds(j, tile, stride=CHUNK_ROWS), :]
        o_ref[:, sl] = x_ref[:, sl] + gate[:, sl] * y


def _combine(xn, ybuf, mods, nb, s, lc):
    ntok, d = xn.shape
    tile = s // 16
    tpb = s // tile
    nt = ntok // tile
    kern = functools.partial(_combine_kernel, tile=tile, tpb=tpb, lc=lc)
    return pl.pallas_call(
        kern,
        grid=(nt,),
        in_specs=[pl.BlockSpec((tile, d), lambda i: (i, 0)),
                  pl.BlockSpec((tile * CHUNK_ROWS, LANES), lambda i: (i, 0)),
                  pl.BlockSpec((tile * CHUNK_ROWS, LANES), lambda i: (nt + i, 0)),
                  pl.BlockSpec((1, 6, d), lambda i: (i // tpb, 0, 0)),
                  pl.BlockSpec((1, 6, d), lambda i: (nb, 0, 0))],
        out_specs=pl.BlockSpec((tile, d), lambda i: (i, 0)),
        out_shape=jax.ShapeDtypeStruct((ntok, d), F32),
        compiler_params=_cparams(("parallel",)),
        name="combine",
    )(xn, ybuf, ybuf, mods, mods)


def _rope_tables(l, lc, rot_dim, lane_off, group):
    n_freq = rot_dim // 4
    inv = jnp.asarray(ROPE_BASE ** (-np.arange(n_freq, dtype=np.float32) / n_freq), dtype=F32)
    t = jnp.arange(l, dtype=jnp.int32)
    row = (t // GRID_W).astype(F32)
    col = (t % GRID_W).astype(F32)
    ang = jnp.concatenate([row[:, None] * inv, col[:, None] * inv], axis=-1)
    cos, sin = jnp.cos(ang), jnp.sin(ang)
    cos_g = jnp.ones((l, group), F32).at[:, lane_off:lane_off + rot_dim].set(jnp.concatenate([cos, cos], -1))
    sin_g = jnp.zeros((l, group), F32).at[:, lane_off:lane_off + rot_dim].set(jnp.concatenate([-sin, sin], -1))
    reps = LANES // group
    cos_t = jnp.tile(cos_g, (1, reps))
    sin_t = jnp.tile(sin_g, (1, reps))
    cos_t = jnp.concatenate([jnp.ones((lc, LANES), F32), cos_t], axis=0)
    sin_t = jnp.concatenate([jnp.zeros((lc, LANES), F32), sin_t], axis=0)
    return cos_t, sin_t


def _na_bias(rpb, rows):
    nq = rows // NA_QROWS
    qi = np.arange(Q_TILE)
    ki = np.arange(NA_KEYS)
    qr_rel, qc = qi // GRID_W, qi % GRID_W
    kr_rel, kc = ki // GRID_W, ki % GRID_W
    win = np.clip(qc - NA_COLS // 2, 0, GRID_W - NA_COLS)
    cols = np.arange(GRID_W)
    dcol = cols[None, :] - cols[:, None] + NA_COLS - 1
    oh_c = (dcol[None] == np.arange(2 * NA_COLS - 1)[:, None, None]).astype(np.float32)
    by_col = jnp.einsum('hrd,dqk->hrqk', rpb.astype(F32), oh_c, precision=HIGHEST)
    tabs = []
    for t in (0, min(1, nq - 1), nq - 1):
        ks = int(np.clip(t * NA_QROWS - NA_ROWS // 2, 0, rows - NA_KROWS))
        qr = t * NA_QROWS + qr_rel
        kr = ks + kr_rel
        r0 = np.clip(qr - NA_ROWS // 2, 0, rows - NA_ROWS)
        ok = ((kr[None, :] >= r0[:, None]) & (kr[None, :] < r0[:, None] + NA_ROWS)
              & (kc[None, :] >= win[:, None]) & (kc[None, :] < win[:, None] + NA_COLS))
        drow = (ks + np.arange(NA_KROWS))[None, :] - (t * NA_QROWS + np.arange(NA_QROWS))[:, None] + NA_ROWS - 1
        oh_r = (drow[None] == np.arange(2 * NA_ROWS - 1)[:, None, None]).astype(np.float32)
        bias = jnp.einsum('rab,hrqk->haqbk', oh_r, by_col, precision=HIGHEST).reshape(-1, Q_TILE, NA_KEYS)
        tabs.append(jnp.where(jnp.asarray(ok)[None], bias, NEG_INF))
    tabs.append(jnp.full_like(tabs[0], NEG_INF))
    return jnp.stack(tabs)


def _layer_weights(w_in, b_w_q_up, b_w_kv_up, w_branch):
    cuts = np.cumsum([512, 128, 128, 512, 256, 32, 512, 512, 512, 512, 512, 512, 512, 512])
    (a_q, a_k, a_v, b_cq, b_ckv, b_kr, c_q, c_i, c_ff, c_fb, c_g, d_q, d_k, d_v, gates) = jnp.split(w_in, cuts, axis=1)
    d = w_in.shape[0]
    aq = a_q.reshape(d, A_HEADS, HEAD_DIM)
    aq = jnp.stack([aq[:, 0:4], aq[:, 4:8]], axis=2).reshape(d, A_HEADS * HEAD_DIM)
    w = jnp.concatenate([aq, a_k, a_v, b_ckv, b_cq, c_q, c_i, c_ff, c_fb, c_g, d_q, d_k, d_v, gates], axis=1)
    wkr = jnp.zeros((d, LANES), F32).at[:, B_NOPE:B_NOPE + B_ROPE].set(b_kr)
    wq = b_w_q_up.reshape(B_Q_RANK, B_HEADS, B_NOPE + B_ROPE)
    wq = jnp.pad(wq, ((0, 0), (0, 0), (0, LANES - B_NOPE - B_ROPE))).reshape(B_Q_RANK, B_HEADS * LANES)
    wkv = b_w_kv_up.reshape(B_KV_RANK, B_HEADS, B_NOPE + B_V)
    wk = jnp.pad(wkv[:, :, :B_NOPE], ((0, 0), (0, 0), (0, LANES - B_NOPE))).reshape(B_KV_RANK, B_HEADS * LANES)
    wv = wkv[:, :, B_NOPE:].reshape(B_KV_RANK, B_HEADS * B_V)
    wb0 = w_branch[0].reshape(A_HEADS, HEAD_DIM, -1)
    wb0 = jnp.stack([wb0[0:4], wb0[4:8]], axis=1).reshape(A_HEADS * HEAD_DIM, -1)
    wb = jnp.concatenate([wb0[None], w_branch[1:]], axis=0)
    cast = lambda t: t.astype(BF16)
    return cast(w), cast(wkr), cast(wq), cast(wk), cast(wv), cast(wb)


def _group_matrix(bounds):
    gid = np.zeros((LANES,), np.int32)
    size = np.zeros((LANES,), np.float32)
    for g, (lo, hi) in enumerate(bounds):
        gid[lo:hi] = g
        size[lo:hi] = hi - lo
    gmat = (gid[:, None] == gid[None, :]).astype(np.float32)
    return jnp.asarray(gmat, dtype=BF16), jnp.asarray(1.0 / size, dtype=F32).reshape(1, LANES)


def _pad_lanes(v, off=0):
    return jnp.zeros((1, LANES), F32).at[0, off:off + v.shape[0]].set(v.astype(F32))


def kernel(x, c, ctx, c_ctx, w_mod, b_mod, g_norm_mix, g_norm_ffn, w_in, a_gq, a_gk, a_sink, b_g_qa, b_g_kva,
           b_w_q_up, b_w_kv_up, b_gq, b_gk, c_lb_logits, c_g_out, d_gq, d_gk, d_rpb, w_branch, w_out,
           router_w, router_b, w_exp_gate, w_exp_up, w_exp_down):
    nb, l, d = x.shape
    lc = ctx.shape[1]
    s = lc + l
    ntok = nb * s
    depth = w_in.shape[0]
    assert l % Q_TILE == 0 and lc % Q_TILE == 0 and (l // GRID_W) >= NA_KROWS and s % 64 == 0

    xs = jnp.concatenate([ctx, x], axis=1).reshape(ntok, d)

    mod_rows = 16
    cvecs = jnp.zeros((mod_rows, d), F32).at[:nb].set(c).at[nb].set(c_ctx)
    mods = _adaln(cvecs, w_mod, b_mod).reshape(depth, mod_rows, 6, d)

    cos_a, sin_a = _rope_tables(l, lc, HEAD_DIM, 0, HEAD_DIM)
    cos_b, sin_b = _rope_tables(l, lc, B_ROPE, B_NOPE, LANES)
    gmat64, _ = _group_matrix([(0, 64), (64, 128)])
    gmat_b, inv_b = _group_matrix([(0, B_NOPE), (B_NOPE, B_NOPE + B_ROPE), (B_NOPE + B_ROPE, LANES)])
    lb_p = jax.nn.softmax(c_lb_logits.astype(F32), axis=1)
    lb_all = jnp.clip(jnp.cumsum(lb_p, axis=1) - lb_p[:, :1], 0.0, 1.0)
    rw_t = router_w.astype(F32).T
    rb = router_b.astype(F32).reshape(N_EXPERTS, 1)

    for lyr in range(depth):
        w, wkr, wq, wk, wv, wb = _layer_weights(w_in[lyr], b_w_q_up[lyr], b_w_kv_up[lyr], w_branch[lyr])
        m = mods[lyr]
        proj, kr = _in_proj(xs, m, g_norm_mix[lyr], w, wkr, nb, s, lc)

        two = lambda g: jnp.concatenate([g, g]).astype(F32)
        gvecs = jnp.stack([two(a_gq[lyr]), two(a_gk[lyr]), two(d_gq[lyr]), two(d_gk[lyr])])
        qk_ad = _prep_ad(proj, cos_a, sin_a, gvecs, gmat64, s)
        qb, kb, vb = _prep_b(proj, kr, wq, wk, wv, b_g_qa[lyr].reshape(1, -1), b_g_kva[lyr].reshape(1, -1),
                             _pad_lanes(b_gq[lyr]), _pad_lanes(b_gk[lyr]), gmat_b, inv_b, cos_b, sin_b, s)

        oa = _attn_a(a_sink[lyr].astype(F32), qk_ad, proj, nb, s, lc)
        ob = _attn_b(qb, kb, vb, nb, s, lc)
        ofw, obw = _gla(proj, lb_all[:, lyr], nb, s, lc)
        od = _attn_d(qk_ad, proj, _na_bias(d_rpb[lyr], l // GRID_W), nb, s, lc)

        y = _branch_mix(oa, ob, ofw, obw, od, proj, wb, c_g_out[lyr].reshape(1, -1).astype(F32), s)
        xn, h2c = _out_proj(y, xs, m, w_out[lyr].astype(BF16), g_norm_ffn[lyr].reshape(1, -1), nb, s, lc)

        idx, wts = _router(h2c, rw_t, rb)
        texp, tval, entries, row_w = _route_plan(idx, wts, ntok)
        ybuf = _experts(h2c, texp, tval, entries, row_w, w_exp_gate[lyr].astype(BF16),
                        w_exp_up[lyr].astype(BF16), w_exp_down[lyr].astype(BF16), ntok)
        xs = _combine(xn, ybuf, m, nb, s, lc)

    return xs.reshape(nb, s, d)[:, lc:]
```

```python
import functools

import numpy as np
import jax
import jax.numpy as jnp
from jax import lax
from jax.experimental import pallas as pl
from jax.experimental.pallas import tpu as pltpu

F32 = jnp.float32
BF16 = jnp.bfloat16
HIGHEST = lax.Precision.HIGHEST

GRID_W = 64
HEAD_DIM = 64
ROPE_BASE = 10000.0
EPS = 1e-6
F_MIN = 1e-6
NEG_INF = -1e30
A_HEADS, A_KV_HEADS, A_WINDOW = 8, 2, 128
B_HEADS, B_NOPE, B_ROPE, B_V, B_Q_RANK, B_KV_RANK = 8, 64, 32, 64, 512, 256
C_HEADS, C_DK, C_DV = 4, 128, 128
D_HEADS, NA_ROWS, NA_COLS = 8, 8, 16
N_BRANCH, BRANCH_W = 4, 512
N_EXPERTS, N_GROUPS, TOP_K, D_EXPERT = 16, 4, 2, 1024

LANES = 128
VMEM_LIMIT = 56 * 1024 * 1024

COL_GATES = 0
COL_AQ, COL_AK, COL_AV, COL_BCKV, COL_BCQ = 8192, 8704, 8832, 8960, 9216
COL_CQ, COL_CI, COL_CFF, COL_CFB, COL_CG = 9728, 10240, 10752, 11264, 11776
COL_DQ, COL_DK, COL_DV = 12288, 12800, 13312
PROJ_W = COL_DV + 512

Q_TILE = 256
GLA_CHUNK = 128
CHUNK_ROWS = 16
EXPERT_TILE = 256


def _cparams(sem, vmem=VMEM_LIMIT):
    return pltpu.CompilerParams(dimension_semantics=sem, vmem_limit_bytes=vmem)


def _sigmoid(x):
    return 1.0 / (1.0 + jnp.exp(-x))


def _silu(x):
    return x * _sigmoid(x)


def _gate_sigmoid(x):
    return 0.5 * jnp.tanh(0.5 * x) + 0.5


def _gate_silu(x):
    return x * _gate_sigmoid(x)


def _rot_pairs(x, half):
    n = x.shape[-1]
    ax = x.ndim - 1
    lane = lax.broadcasted_iota(jnp.int32, x.shape, ax)
    fwd = pltpu.roll(x, n - half, axis=ax)
    bwd = pltpu.roll(x, half, axis=ax)
    return jnp.where((lane & (2 * half - 1)) < half, fwd, bwd)


def _row_mod(i, mb_ref, mc_ref, idx, tile, tiles_per_batch, lc):
    row = (i % tiles_per_batch) * tile + lax.broadcasted_iota(jnp.int32, (tile, 1), 0)
    return jnp.where(row < lc, mc_ref[0, idx:idx + 1, :], mb_ref[0, idx:idx + 1, :])


def _mod_norm(x, g, shift, scale):
    ms = jnp.mean(x * x, axis=-1, keepdims=True)
    return x * lax.rsqrt(ms + EPS) * g * (1.0 + scale) + shift


def _adaln_kernel(c_ref, w_ref, b_ref, o_ref):
    cv = c_ref[...]
    o_ref[0] = jnp.dot(_silu(cv), w_ref[0], preferred_element_type=F32, precision=HIGHEST) + b_ref[0]


def _adaln(cvecs, w_mod, b_mod):
    depth, d, n6 = w_mod.shape
    rows = cvecs.shape[0]
    tn = 1024
    return pl.pallas_call(
        _adaln_kernel,
        grid=(depth, n6 // tn),
        in_specs=[pl.BlockSpec((rows, d), lambda l, j: (0, 0)),
                  pl.BlockSpec((1, d, tn), lambda l, j: (l, 0, j)),
                  pl.BlockSpec((1, 1, tn), lambda l, j: (l, 0, j))],
        out_specs=pl.BlockSpec((1, rows, tn), lambda l, j: (l, 0, j)),
        out_shape=jax.ShapeDtypeStruct((depth, rows, n6), F32),
        compiler_params=_cparams(("parallel", "parallel")),
        name="adaln",
    )(cvecs, w_mod, b_mod.reshape(depth, 1, n6))


def _inproj_kernel(x_ref, mb_ref, mc_ref, g_ref, w_ref, wkr_ref, o_ref, kr_ref, h_scr, *, tile, tpb, lc):
    i = pl.program_id(0)

    @pl.when(pl.program_id(1) == 0)
    def _():
        shift = _row_mod(i, mb_ref, mc_ref, 0, tile, tpb, lc)
        scale = _row_mod(i, mb_ref, mc_ref, 1, tile, tpb, lc)
        hb = _mod_norm(x_ref[...], g_ref[...], shift, scale).astype(BF16)
        h_scr[...] = hb
        kr_ref[...] = jnp.dot(hb, wkr_ref[...], preferred_element_type=F32).astype(BF16)

    o_ref[...] = jnp.dot(h_scr[...], w_ref[...], preferred_element_type=F32).astype(BF16)


def _in_proj(xs, mods, g, w, wkr, nb, s, lc):
    ntok, d = xs.shape
    tile = s // 4
    tpb = s // tile
    tn = 1536
    kern = functools.partial(_inproj_kernel, tile=tile, tpb=tpb, lc=lc)
    return pl.pallas_call(
        kern,
        grid=(ntok // tile, PROJ_W // tn),
        in_specs=[pl.BlockSpec((tile, d), lambda i, j: (i, 0)),
                  pl.BlockSpec((1, 6, d), lambda i, j: (i // tpb, 0, 0)),
                  pl.BlockSpec((1, 6, d), lambda i, j: (nb, 0, 0)),
                  pl.BlockSpec((1, d), lambda i, j: (0, 0)),
                  pl.BlockSpec((d, tn), lambda i, j: (0, j)),
                  pl.BlockSpec((d, LANES), lambda i, j: (0, 0))],
        out_specs=[pl.BlockSpec((tile, tn), lambda i, j: (i, j)),
                   pl.BlockSpec((tile, LANES), lambda i, j: (i, 0))],
        out_shape=[jax.ShapeDtypeStruct((ntok, PROJ_W), BF16),
                   jax.ShapeDtypeStruct((ntok, LANES), BF16)],
        scratch_shapes=[pltpu.VMEM((tile, d), BF16)],
        compiler_params=_cparams(("parallel", "arbitrary")),
        name="in_proj",
    )(xs, mods, mods, g.reshape(1, d), w, wkr)


def _group_norm(x, gmat, inv_size, gvec):
    ss = jnp.dot((x * x).astype(BF16), gmat, preferred_element_type=F32)
    return x * lax.rsqrt(ss * inv_size + EPS) * gvec


def _prep_ad_kernel(aq_ref, ak_ref, d_ref, cos_ref, sin_ref, g_ref, gmat_ref, o_ref):
    gmat = gmat_ref[...]
    cos = cos_ref[...]
    sin = sin_ref[...]
    scale = HEAD_DIM ** -0.5
    inv = 1.0 / HEAD_DIM
    for blk in range(5):
        x = (aq_ref[:, blk * LANES:(blk + 1) * LANES] if blk < 4 else ak_ref[...]).astype(F32)
        gi = 0 if blk < 4 else 1
        xn = _group_norm(x, gmat, inv, g_ref[gi:gi + 1, :])
        xr = xn * cos + _rot_pairs(xn, HEAD_DIM // 2) * sin
        if blk < 4:
            xr = xr * scale
        ob = blk if blk < 4 else 12
        o_ref[:, ob * LANES:(ob + 1) * LANES] = xr.astype(BF16)
    for blk in range(8):
        x = d_ref[:, blk * LANES:(blk + 1) * LANES].astype(F32)
        gi = 2 if blk < 4 else 3
        xn = _group_norm(x, gmat, inv, g_ref[gi:gi + 1, :])
        if blk < 4:
            xn = xn * scale
        o_ref[:, (4 + blk) * LANES:(5 + blk) * LANES] = xn.astype(BF16)


def _prep_ad(proj, cos, sin, gvecs, gmat, s):
    ntok = proj.shape[0]
    tile = s // 4
    tpb = s // tile
    wa, wd = 4 * LANES, 8 * LANES
    return pl.pallas_call(
        _prep_ad_kernel,
        grid=(ntok // tile,),
        in_specs=[pl.BlockSpec((tile, wa), lambda i: (i, COL_AQ // wa)),
                  pl.BlockSpec((tile, LANES), lambda i: (i, COL_AK // LANES)),
                  pl.BlockSpec((tile, wd), lambda i: (i, COL_DQ // wd)),
                  pl.BlockSpec((tile, LANES), lambda i: (i % tpb, 0)),
                  pl.BlockSpec((tile, LANES), lambda i: (i % tpb, 0)),
                  pl.BlockSpec((4, LANES), lambda i: (0, 0)),
                  pl.BlockSpec((LANES, LANES), lambda i: (0, 0))],
        out_specs=pl.BlockSpec((tile, 13 * LANES), lambda i: (i, 0)),
        out_shape=jax.ShapeDtypeStruct((ntok, 13 * LANES), BF16),
        compiler_params=_cparams(("parallel",)),
        name="prep_ad",
    )(proj, proj, proj, cos, sin, gvecs, gmat)


def _prep_b_kernel(cq_ref, ckv_ref, kr_ref, wq_ref, wk_ref, wv_ref, gqa_ref, gkva_ref, gq_ref, gk_ref,
                   gmat_ref, inv_ref, cos_ref, sin_ref, q_ref, k_ref, v_ref):
    gmat = gmat_ref[...]
    inv = inv_ref[...]
    cos = cos_ref[...]
    sin = sin_ref[...]
    scale = (B_NOPE + B_ROPE) ** -0.5

    def rms(x, g):
        return x * lax.rsqrt(jnp.mean(x * x, axis=-1, keepdims=True) + EPS) * g

    cq = rms(cq_ref[...].astype(F32), gqa_ref[...]).astype(BF16)
    ckv = rms(ckv_ref[...].astype(F32), gkva_ref[...]).astype(BF16)
    q = jnp.dot(cq, wq_ref[...], preferred_element_type=F32)
    kn = jnp.dot(ckv, wk_ref[...], preferred_element_type=F32)
    v_ref[...] = jnp.dot(ckv, wv_ref[...], preferred_element_type=F32).astype(BF16)
    kro = _group_norm(kr_ref[...].astype(F32), gmat, inv, gk_ref[...])
    kro = kro * cos + _rot_pairs(kro, B_ROPE // 2) * sin
    for h in range(B_HEADS):
        sl = slice(h * LANES, (h + 1) * LANES)
        qh = _group_norm(q[:, sl], gmat, inv, gq_ref[...])
        qh = (qh * cos + _rot_pairs(qh, B_ROPE // 2) * sin) * scale
        q_ref[:, sl] = qh.astype(BF16)
        kh = _group_norm(kn[:, sl], gmat, inv, gk_ref[...]) + kro
        k_ref[:, sl] = kh.astype(BF16)


def _prep_b(proj, kr, wq, wk, wv, gqa, gkva, gq, gk, gmat, inv, cos, sin, s):
    ntok = proj.shape[0]
    tile = s // 4
    tpb = s // tile
    hw = B_HEADS * LANES
    full = lambda shape: pl.BlockSpec(shape, lambda i: (0, 0))
    return pl.pallas_call(
        _prep_b_kernel,
        grid=(ntok // tile,),
        in_specs=[pl.BlockSpec((tile, B_Q_RANK), lambda i: (i, COL_BCQ // B_Q_RANK)),
                  pl.BlockSpec((tile, B_KV_RANK), lambda i: (i, COL_BCKV // B_KV_RANK)),
                  pl.BlockSpec((tile, LANES), lambda i: (i, 0)),
                  full((B_Q_RANK, hw)), full((B_KV_RANK, hw)), full((B_KV_RANK, B_HEADS * B_V)),
                  full((1, B_Q_RANK)), full((1, B_KV_RANK)), full((1, LANES)), full((1, LANES)),
                  full((LANES, LANES)), full((1, LANES)),
                  pl.BlockSpec((tile, LANES), lambda i: (i % tpb, 0)),
                  pl.BlockSpec((tile, LANES), lambda i: (i % tpb, 0))],
        out_specs=[pl.BlockSpec((tile, hw), lambda i: (i, 0)),
                   pl.BlockSpec((tile, hw), lambda i: (i, 0)),
                   pl.BlockSpec((tile, B_HEADS * B_V), lambda i: (i, 0))],
        out_shape=[jax.ShapeDtypeStruct((ntok, hw), BF16),
                   jax.ShapeDtypeStruct((ntok, hw), BF16),
                   jax.ShapeDtypeStruct((ntok, B_HEADS * B_V), BF16)],
        compiler_params=_cparams(("parallel",)),
        name="prep_b",
    )(proj, proj, kr, wq, wk, wv, gqa, gkva, gq, gk, gmat, inv, cos, sin)


def _nt_dot(a, b):
    return lax.dot_general(a, b, (((1,), (1,)), ((), ())), preferred_element_type=F32)


def _half_mask(shape, half):
    lane = lax.broadcasted_iota(jnp.int32, shape, len(shape) - 1)
    return (lane >= HEAD_DIM) if half else (lane < HEAD_DIM)


def _joint_attend(s, sc, vw, vc, sink=None):
    m = jnp.maximum(jnp.max(s, axis=-1, keepdims=True), jnp.max(sc, axis=-1, keepdims=True))
    if sink is not None:
        m = jnp.maximum(m, sink)
    p = jnp.exp(s - m)
    pc = jnp.exp(sc - m)
    den = jnp.sum(p, axis=-1, keepdims=True) + jnp.sum(pc, axis=-1, keepdims=True)
    if sink is not None:
        den = den + jnp.exp(sink - m)
    o = (jnp.dot(p.astype(BF16), vw, preferred_element_type=F32)
         + jnp.dot(pc.astype(BF16), vc, preferred_element_type=F32))
    return o / den


def _attn_a_kernel(sink_ref, q_ref, k_ref, v_ref, o_ref, *, lc, l):
    qt = pl.program_id(1)
    span = Q_TILE + 2 * A_WINDOW
    qstart = (qt - 1) * Q_TILE
    ws = jnp.clip(qstart - A_WINDOW, 0, l - span)
    wrow = pl.multiple_of(lc + ws, LANES)
    kw = k_ref[pl.ds(wrow, span), :]
    vw = v_ref[pl.ds(wrow, span), :]
    kc = k_ref[0:lc, :]
    vc = v_ref[0:lc, :]
    kpos = ws + lax.broadcasted_iota(jnp.int32, (Q_TILE, span), 1)
    qoff = jnp.where(qt > 0, qstart, -(1 << 20))
    qpos = qoff + lax.broadcasted_iota(jnp.int32, (Q_TILE, span), 0)
    ok = jnp.abs(kpos - qpos) <= A_WINDOW
    for blk in range(4):
        qb = q_ref[:, blk * LANES:(blk + 1) * LANES]
        halves = []
        for half in range(2):
            head = blk + 4 * half
            qm = jnp.where(_half_mask(qb.shape, half), qb, jnp.zeros_like(qb))
            s = jnp.where(ok, _nt_dot(qm, kw), NEG_INF)
            sc = _nt_dot(qm, kc)
            halves.append(_joint_attend(s, sc, vw, vc, sink_ref[head]))
        out = jnp.where(_half_mask(halves[0].shape, 0), halves[0], halves[1])
        o_ref[:, blk * LANES:(blk + 1) * LANES] = out.astype(BF16)


def _attn_a(sink, qk, proj, nb, s, lc):
    ntok = qk.shape[0]
    nq = s // Q_TILE
    kern = functools.partial(_attn_a_kernel, lc=lc, l=s - lc)
    return pl.pallas_call(
        kern,
        grid_spec=pltpu.PrefetchScalarGridSpec(
            num_scalar_prefetch=0,
            grid=(nb, nq),
            in_specs=[pl.BlockSpec(memory_space=pltpu.SMEM),
                      pl.BlockSpec((Q_TILE, 4 * LANES), lambda b, t: (b * nq + t, 0)),
                      pl.BlockSpec((s, LANES), lambda b, t: (b, 12)),
                      pl.BlockSpec((s, LANES), lambda b, t: (b, COL_AV // LANES))],
            out_specs=pl.BlockSpec((Q_TILE, 4 * LANES), lambda b, t: (b * nq + t, 0)),
        ),
        out_shape=jax.ShapeDtypeStruct((ntok, 4 * LANES), BF16),
        compiler_params=_cparams(("parallel", "arbitrary")),
        name="attn_a",
    )(sink, qk, qk, proj)


def _attn_b_kernel(q_ref, k_ref, v_ref, o_ref, s_scr, p_scr, *, s, lc):
    qt = pl.program_id(2)

    def head(hh, nkeys):
        hl = slice(hh * LANES, (hh + 1) * LANES)
        q = q_ref[:, hl]
        for c in range(nkeys // Q_TILE):
            rows = slice(c * Q_TILE, (c + 1) * Q_TILE)
            s_scr[hh, :, rows] = _nt_dot(q, k_ref[rows, hl])
        mx = s_scr[hh, :, 0:LANES]
        for j in range(1, nkeys // LANES):
            mx = jnp.maximum(mx, s_scr[hh, :, j * LANES:(j + 1) * LANES])
        m = jnp.broadcast_to(jnp.max(mx, axis=-1, keepdims=True), (Q_TILE, LANES))
        lsum = jnp.zeros((Q_TILE, LANES), F32)
        for j in range(nkeys // LANES):
            cols = slice(j * LANES, (j + 1) * LANES)
            p = jnp.exp(s_scr[hh, :, cols] - m)
            lsum = lsum + p
            p_scr[hh, :, cols] = p.astype(BF16)
        den = jnp.sum(lsum, axis=-1, keepdims=True)
        o = jnp.dot(p_scr[hh, :, 0:nkeys], v_ref[0:nkeys, :], preferred_element_type=F32)
        return o / den

    def both(nkeys):
        r0 = head(0, nkeys)
        r1 = head(1, nkeys)
        o_ref[...] = jnp.where(_half_mask(r0.shape, 0), r0, r1).astype(BF16)

    @pl.when(qt == 0)
    def _():
        both(lc)

    @pl.when(qt > 0)
    def _():
        both(s)


def _attn_b(qb, kb, vb, nb, s, lc):
    ntok = qb.shape[0]
    nq = s // Q_TILE
    kern = functools.partial(_attn_b_kernel, s=s, lc=lc)
    return pl.pallas_call(
        kern,
        grid=(nb, B_HEADS // 2, nq),
        in_specs=[pl.BlockSpec((Q_TILE, 2 * LANES), lambda b, p, t: (b * nq + t, p)),
                  pl.BlockSpec((s, 2 * LANES), lambda b, p, t: (b, p)),
                  pl.BlockSpec((s, LANES), lambda b, p, t: (b, p))],
        out_specs=pl.BlockSpec((Q_TILE, LANES), lambda b, p, t: (b * nq + t, p)),
        out_shape=jax.ShapeDtypeStruct((ntok, B_HEADS * B_V), BF16),
        scratch_shapes=[pltpu.VMEM((2, Q_TILE, s), F32), pltpu.VMEM((2, Q_TILE, s), BF16)],
        compiler_params=_cparams(("parallel", "parallel", "arbitrary")),
        name="attn_b",
    )(qb, kb, vb)


NA_QROWS = Q_TILE // GRID_W
NA_KROWS = NA_QROWS + NA_ROWS
NA_KEYS = NA_KROWS * GRID_W


def _attn_d_kernel(q_ref, k_ref, v_ref, bias_ref, o_ref, *, lc, rows):
    qt = pl.program_id(1)
    ks = jnp.clip((qt - 1) * NA_QROWS - NA_ROWS // 2, 0, rows - NA_KROWS)
    wrow = pl.multiple_of(lc + ks * GRID_W, Q_TILE)
    for blk in range(4):
        sl = slice(blk * LANES, (blk + 1) * LANES)
        qb = q_ref[:, sl]
        kw = k_ref[pl.ds(wrow, NA_KEYS), sl]
        vw = v_ref[pl.ds(wrow, NA_KEYS), sl]
        kc = k_ref[0:lc, sl]
        vc = v_ref[0:lc, sl]
        halves = []
        for half in range(2):
            head = 2 * blk + half
            qm = jnp.where(_half_mask(qb.shape, half), qb, jnp.zeros_like(qb))
            s = _nt_dot(qm, kw) + bias_ref[0, head]
            sc = _nt_dot(qm, kc)
            halves.append(_joint_attend(s, sc, vw, vc))
        out = jnp.where(_half_mask(halves[0].shape, 0), halves[0], halves[1])
        o_ref[:, sl] = out.astype(BF16)


def _na_variant(t, nq):
    return jnp.where(t == 0, 3, jnp.where(t == 1, 0, jnp.where(t == nq - 1, 2, 1)))


def _attn_d(qk, proj, bias, nb, s, lc):
    ntok = qk.shape[0]
    nq = s // Q_TILE
    rows = (s - lc) // GRID_W
    kern = functools.partial(_attn_d_kernel, lc=lc, rows=rows)
    w = 4 * LANES
    return pl.pallas_call(
        kern,
        grid=(nb, nq),
        in_specs=[pl.BlockSpec((Q_TILE, w), lambda b, t: (b * nq + t, 1)),
                  pl.BlockSpec((s, w), lambda b, t: (b, 2)),
                  pl.BlockSpec((s, w), lambda b, t: (b, COL_DV // w)),
                  pl.BlockSpec((1, D_HEADS, Q_TILE, NA_KEYS), lambda b, t: (_na_variant(t, nq), 0, 0, 0))],
        out_specs=pl.BlockSpec((Q_TILE, w), lambda b, t: (b * nq + t, 0)),
        out_shape=jax.ShapeDtypeStruct((ntok, w), BF16),
        compiler_params=_cparams(("parallel", "arbitrary")),
        name="attn_d",
    )(qk, qk, proj, bias)


def _cumsum_rows(x, reverse):
    n = x.shape[0]
    row = lax.broadcasted_iota(jnp.int32, x.shape, 0)
    step = 1
    while step < n:
        if reverse:
            x = x + jnp.where(row < n - step, pltpu.roll(x, n - step, axis=0), 0.0)
        else:
            x = x + jnp.where(row >= step, pltpu.roll(x, step, axis=0), 0.0)
        step *= 2
    return x


def _block_ref(b, size, idx):
    n, w = b.shape
    b3 = b.reshape(n // size, size, w)
    return jnp.broadcast_to(b3[:, idx:idx + 1, :], b3.shape).reshape(n, w)


def _gla_chunk(q, v, fpre, lb, st_ref, reverse):
    c = q.shape[0]
    fg = lb + (1.0 - lb) * _sigmoid(fpre)
    logf = jnp.log(jnp.maximum(fg, F_MIN))
    k = 1.0 - fg
    qs = _silu(q)
    b = _cumsum_rows(logf, reverse)
    row = lax.broadcasted_iota(jnp.int32, (c, c), 0)
    col = lax.broadcasted_iota(jnp.int32, (c, c), 1)
    xor = row ^ col
    after = (row < col) if reverse else (row > col)

    base = 8
    bref = _block_ref(b, base, base // 2 if reverse else base // 2 - 1)
    qe = (qs * jnp.exp(b - bref)).astype(BF16)
    ke = (k * jnp.exp(bref - b)).astype(BF16)
    a = jnp.where((xor < base) & (after | (row == col)), _nt_dot(qe, ke), 0.0)
    m = base
    while m < c:
        bref = _block_ref(b, 2 * m, m if reverse else m - 1)
        qe = (qs * jnp.exp(jnp.minimum(b - bref, 0.0))).astype(BF16)
        ke = (k * jnp.exp(jnp.minimum(bref - b, 0.0))).astype(BF16)
        a = jnp.where((xor >= m) & (xor < 2 * m) & after, _nt_dot(qe, ke), a)
        m *= 2

    st = st_ref[...]
    vb = v.astype(BF16)
    o = (jnp.dot(a.astype(BF16), vb, preferred_element_type=F32)
         + _nt_dot((qs * jnp.exp(b)).astype(BF16), st.astype(BF16)))
    btot = b[0:1, :] if reverse else b[c - 1:c, :]
    kdec = (k * jnp.exp(btot - b)).astype(BF16)
    upd = lax.dot_general(vb, kdec, (((0,), (0,)), ((), ())), preferred_element_type=F32)
    st_ref[...] = st * jnp.exp(btot) + upd
    return o


def _gla_kernel(qf_ref, vf_ref, ff_ref, qb_ref, vb_ref, fb_ref, lb_ref, of_ref, ob_ref, stf, stb):
    @pl.when(pl.program_id(1) == 0)
    def _():
        stf[...] = jnp.zeros_like(stf)
        stb[...] = jnp.zeros_like(stb)

    for h in range(C_HEADS):
        sl = slice(h * LANES, (h + 1) * LANES)
        of_ref[:, sl] = _gla_chunk(qf_ref[:, sl].astype(F32), vf_ref[:, sl].astype(F32),
                                   ff_ref[:, sl].astype(F32), lb_ref[0:1, sl], stf.at[h], False)
        ob_ref[:, sl] = _gla_chunk(qb_ref[:, sl].astype(F32), vb_ref[:, sl].astype(F32),
                                   fb_ref[:, sl].astype(F32), lb_ref[1:2, sl], stb.at[h], True)


def _gla(proj, lb, nb, s, lc):
    ntok = proj.shape[0]
    c = GLA_CHUNK
    ns = s // c
    nctx = lc // c
    w = C_HEADS * C_DK

    def bw_chunk(t):
        return jnp.where(t < nctx, nctx - 1 - t, ns - 1 - (t - nctx))

    def fw(col):
        return pl.BlockSpec((c, w), lambda b, t: (b * ns + t, col // w))

    def bw(col):
        return pl.BlockSpec((c, w), lambda b, t: (b * ns + bw_chunk(t), col // w))

    return pl.pallas_call(
        _gla_kernel,
        grid=(nb, ns),
        in_specs=[fw(COL_CQ), fw(COL_CI), fw(COL_CFF), bw(COL_CQ), bw(COL_CI), bw(COL_CFB),
                  pl.BlockSpec((2, w), lambda b, t: (0, 0))],
        out_specs=[pl.BlockSpec((c, w), lambda b, t: (b * ns + t, 0)),
                   pl.BlockSpec((c, w), lambda b, t: (b * ns + bw_chunk(t), 0))],
        out_shape=[jax.ShapeDtypeStruct((ntok, C_HEADS * C_DV), F32)] * 2,
        scratch_shapes=[pltpu.VMEM((C_HEADS, C_DV, C_DK), F32), pltpu.VMEM((C_HEADS, C_DV, C_DK), F32)],
        compiler_params=_cparams(("parallel", "arbitrary")),
        name="gla",
    )(proj, proj, proj, proj, proj, proj, lb)


def _mix_kernel(oa_ref, ob_ref, ofw_ref, obw_ref, cg_ref, od_ref, g0_ref, g1_ref, g2_ref, g3_ref,
                wb_ref, gout_ref, y_ref, oc_scr):
    @pl.when(pl.program_id(1) == 0)
    def _():
        gout = gout_ref[...]
        for h in range(C_HEADS):
            sl = slice(h * LANES, (h + 1) * LANES)
            o = ofw_ref[:, sl] + obw_ref[:, sl]
            on = o * lax.rsqrt(jnp.mean(o * o, axis=-1, keepdims=True) + EPS) * gout
            oc_scr[:, sl] = (on * _gate_silu(cg_ref[:, sl].astype(F32))).astype(BF16)

    outs = (oa_ref[...], ob_ref[...], oc_scr[...], od_ref[...])
    gates = (g0_ref, g1_ref, g2_ref, g3_ref)
    y = None
    for br in range(N_BRANCH):
        t = _gate_sigmoid(gates[br][...].astype(F32)) * jnp.dot(outs[br], wb_ref[br], preferred_element_type=F32)
        y = t if y is None else y + t
    y_ref[...] = y.astype(BF16)


def _branch_mix(oa, ob, ofw, obw, od, proj, wb, gout, s):
    ntok = oa.shape[0]
    d = wb.shape[2]
    tile = s // 8
    tn = 1024
    row512 = pl.BlockSpec((tile, 512), lambda i, n: (i, 0))

    def gate(br):
        return pl.BlockSpec((tile, tn), lambda i, n: (i, (COL_GATES + br * d) // tn + n))

    return pl.pallas_call(
        _mix_kernel,
        grid=(ntok // tile, d // tn),
        in_specs=[row512, row512, row512, row512,
                  pl.BlockSpec((tile, 512), lambda i, n: (i, COL_CG // 512)),
                  row512, gate(0), gate(1), gate(2), gate(3),
                  pl.BlockSpec((N_BRANCH, BRANCH_W, tn), lambda i, n: (0, 0, n)),
                  pl.BlockSpec((1, LANES), lambda i, n: (0, 0))],
        out_specs=pl.BlockSpec((tile, tn), lambda i, n: (i, n)),
        out_shape=jax.ShapeDtypeStruct((ntok, d), BF16),
        scratch_shapes=[pltpu.VMEM((tile, 512), BF16)],
        compiler_params=_cparams(("parallel", "arbitrary")),
        name="branch_mix",
    )(oa, ob, ofw, obw, proj, od, proj, proj, proj, proj, wb, gout)


def _outproj_kernel(y_ref, x_ref, mb_ref, mc_ref, wo_ref, gffn_ref, xn_ref, h2_ref, *, tile, tpb, lc):
    i = pl.program_id(0)
    out = jnp.dot(y_ref[...], wo_ref[...], preferred_element_type=F32)
    xn = x_ref[...] + _row_mod(i, mb_ref, mc_ref, 2, tile, tpb, lc) * out
    xn_ref[...] = xn
    h2 = _mod_norm(xn, gffn_ref[...], _row_mod(i, mb_ref, mc_ref, 3, tile, tpb, lc),
                   _row_mod(i, mb_ref, mc_ref, 4, tile, tpb, lc))
    for j in range(CHUNK_ROWS):
        h2_ref[pl.ds(j, tile, stride=CHUNK_ROWS), :] = h2[:, j * LANES:(j + 1) * LANES]


def _out_proj(y, xs, mods, wo, gffn, nb, s, lc):
    ntok, d = xs.shape
    tile = s // 8
    tpb = s // tile
    kern = functools.partial(_outproj_kernel, tile=tile, tpb=tpb, lc=lc)
    return pl.pallas_call(
        kern,
        grid=(ntok // tile,),
        in_specs=[pl.BlockSpec((tile, d), lambda i: (i, 0)),
                  pl.BlockSpec((tile, d), lambda i: (i, 0)),
                  pl.BlockSpec((1, 6, d), lambda i: (i // tpb, 0, 0)),
                  pl.BlockSpec((1, 6, d), lambda i: (nb, 0, 0)),
                  pl.BlockSpec((d, d), lambda i: (0, 0), pipeline_mode=pl.Buffered(1)),
                  pl.BlockSpec((1, d), lambda i: (0, 0))],
        out_specs=[pl.BlockSpec((tile, d), lambda i: (i, 0)),
                   pl.BlockSpec((tile * CHUNK_ROWS, LANES), lambda i: (i, 0))],
        out_shape=[jax.ShapeDtypeStruct((ntok, d), F32),
                   jax.ShapeDtypeStruct((ntok * CHUNK_ROWS, LANES), F32)],
        compiler_params=_cparams(("parallel",)),
        name="out_proj",
    )(y, xs, mods, mods, wo, gffn)


def _router_kernel(h_ref, rw_ref, rb_ref, idx_ref, w_ref, *, tile):
    logits = None
    for j in range(CHUNK_ROWS):
        hj = h_ref[pl.ds(j, tile, stride=CHUNK_ROWS), :]
        part = lax.dot_general(rw_ref[:, j * LANES:(j + 1) * LANES], hj, (((1,), (1,)), ((), ())),
                               preferred_element_type=F32, precision=HIGHEST)
        logits = part if logits is None else logits + part
    scores = _sigmoid(logits)
    biased = scores + rb_ref[...]
    sc = [scores[e:e + 1, :] for e in range(N_EXPERTS)]
    bi = [biased[e:e + 1, :] for e in range(N_EXPERTS)]
    per = N_EXPERTS // N_GROUPS
    best_g = jnp.zeros((1, tile), jnp.int32)
    best_s = None
    for g in range(N_GROUPS):
        members = bi[g * per:(g + 1) * per]
        gs = None
        for a in range(per):
            for b in range(a + 1, per):
                pair = members[a] + members[b]
                gs = pair if gs is None else jnp.maximum(gs, pair)
        if best_s is None:
            best_s = gs
        else:
            better = gs > best_s
            best_g = jnp.where(better, g, best_g)
            best_s = jnp.where(better, gs, best_s)
    picks = []
    taken = jnp.full((1, tile), -1, jnp.int32)
    for _ in range(TOP_K):
        bv = jnp.full((1, tile), -3e38, F32)
        be = jnp.zeros((1, tile), jnp.int32)
        bw = jnp.zeros((1, tile), F32)
        for e in range(N_EXPERTS):
            val = jnp.where(best_g == e // per, bi[e], NEG_INF)
            better = (val > bv) & (taken != e)
            bv = jnp.where(better, val, bv)
            be = jnp.where(better, e, be)
            bw = jnp.where(better, sc[e], bw)
        picks.append((be, bw))
        taken = be
    tot = picks[0][1] + picks[1][1]
    idx_ref[...] = jnp.concatenate([picks[0][0], picks[1][0]], axis=0)
    w_ref[...] = jnp.concatenate([picks[0][1] / tot, picks[1][1] / tot], axis=0)


def _router(h2c, rw_t, rb):
    ntok = h2c.shape[0] // CHUNK_ROWS
    tile = 1024 if ntok % 1024 == 0 else 256
    d = rw_t.shape[1]
    kern = functools.partial(_router_kernel, tile=tile)
    return pl.pallas_call(
        kern,
        grid=(ntok // tile,),
        in_specs=[pl.BlockSpec((tile * CHUNK_ROWS, LANES), lambda i: (i, 0)),
                  pl.BlockSpec((N_EXPERTS, d), lambda i: (0, 0)),
                  pl.BlockSpec((N_EXPERTS, 1), lambda i: (0, 0))],
        out_specs=[pl.BlockSpec((TOP_K, tile), lambda i: (0, i)),
                   pl.BlockSpec((TOP_K, tile), lambda i: (0, i))],
        out_shape=[jax.ShapeDtypeStruct((TOP_K, ntok), jnp.int32),
                   jax.ShapeDtypeStruct((TOP_K, ntok), F32)],
        compiler_params=_cparams(("parallel",)),
        name="router",
    )(h2c, rw_t, rb)


def _expert_kernel(texp_ref, ent_ref, h_hbm, rw_ref, wg_ref, wu_ref, wd_ref, y_hbm,
                   gbuf0, gbuf1, xbuf, obuf0, obuf1, gsem, ssem, *, ntok, tile):
    i = pl.program_id(0)
    n = pl.num_programs(0)
    slab = tile * CHUNK_ROWS
    gbufs = (gbuf0, gbuf1)
    obufs = (obuf0, obuf1)

    def rows(r):
        return pl.ds(pl.multiple_of(r * CHUNK_ROWS, CHUNK_ROWS), CHUNK_ROWS)

    def gather_row(t, r, sl):
        e = ent_ref[(t + 1) * tile + r]
        src = jnp.where(e >= 2 * ntok, 0, jnp.where(e >= ntok, e - ntok, e))
        pltpu.make_async_copy(h_hbm.at[rows(src), :], gbufs[sl].at[rows(r), :], gsem.at[sl]).start()

    def scatter_row(t, r, sl):
        e = ent_ref[(t + 1) * tile + r]
        pltpu.make_async_copy(obufs[sl].at[rows(r), :], y_hbm.at[rows(e), :], ssem.at[sl]).start()

    def wait_gather(sl):
        pltpu.make_async_copy(h_hbm.at[pl.ds(0, slab), :], gbufs[sl], gsem.at[sl]).wait()

    def wait_scatter(sl):
        pltpu.make_async_copy(obufs[sl], y_hbm.at[pl.ds(0, slab), :], ssem.at[sl]).wait()

    def looped(fn):
        def body(r, carry):
            fn(r)
            return carry
        lax.fori_loop(0, tile, body, 0, unroll=8)

    def step(sl):
        other = 1 - sl

        @pl.when(i == 0)
        def _():
            obufs[other][...] = jnp.zeros((slab, LANES), F32)
            looped(lambda r: gather_row(0, r, sl))

        wait_gather(sl)

        def issue(part):
            lo, hi = part * tile // 4, (part + 1) * tile // 4
            for r in range(lo, hi):
                gather_row(i + 1, r, other)
                scatter_row(i - 1, r, other)

        issue(0)
        for j in range(CHUNK_ROWS):
            xbuf[:, j * LANES:(j + 1) * LANES] = gbufs[sl][pl.ds(j, tile, stride=CHUNK_ROWS), :].astype(BF16)
        x = xbuf[...]
        issue(1)
        hg = jnp.dot(x, wg_ref[0], preferred_element_type=F32)
        issue(2)
        hu = jnp.dot(x, wu_ref[0], preferred_element_type=F32)
        he = (_gate_silu(hg) * hu).astype(BF16)
        issue(3)
        y = jnp.dot(he, wd_ref[0], preferred_element_type=F32) * rw_ref[...]

        @pl.when(i >= 1)
        def _():
            wait_scatter(sl)

        for j in range(CHUNK_ROWS):
            obufs[sl][pl.ds(j, tile, stride=CHUNK_ROWS), :] = y[:, j * LANES:(j + 1) * LANES]

        @pl.when(i == n - 1)
        def _():
            looped(lambda r: scatter_row(i, r, sl))
            wait_scatter(other)
            wait_scatter(sl)
            wait_gather(other)

    @pl.when(i % 2 == 0)
    def _():
        step(0)

    @pl.when(i % 2 == 1)
    def _():
        step(1)


def _experts(h2c, tile_exp, entries, row_w, wg, wu, wd, ntok):
    tile = EXPERT_TILE
    p = entries.shape[0]
    d = wg.shape[1]
    de = wg.shape[2]
    lead = p + jnp.arange(tile, dtype=jnp.int32)
    trail = jnp.full((tile,), TOP_K * ntok, jnp.int32)
    plan = jnp.concatenate([lead, entries, trail])
    slab = pltpu.VMEM((tile * CHUNK_ROWS, LANES), F32)
    kern = functools.partial(_expert_kernel, ntok=ntok, tile=tile)
    return pl.pallas_call(
        kern,
        grid_spec=pltpu.PrefetchScalarGridSpec(
            num_scalar_prefetch=2,
            grid=(p // tile,),
            in_specs=[pl.BlockSpec(memory_space=pl.ANY),
                      pl.BlockSpec((tile, 1), lambda i, te, en: (i, 0)),
                      pl.BlockSpec((1, d, de), lambda i, te, en: (te[i], 0, 0)),
                      pl.BlockSpec((1, d, de), lambda i, te, en: (te[i], 0, 0)),
                      pl.BlockSpec((1, de, d), lambda i, te, en: (te[i], 0, 0))],
            out_specs=pl.BlockSpec(memory_space=pl.ANY),
            scratch_shapes=[slab, slab, pltpu.VMEM((tile, d), BF16), slab, slab,
                            pltpu.SemaphoreType.DMA((2,)),
                            pltpu.SemaphoreType.DMA((2,))]),
        out_shape=jax.ShapeDtypeStruct(((p + tile) * CHUNK_ROWS, LANES), F32),
        compiler_params=_cparams(("arbitrary",)),
        name="experts",
    )(tile_exp, plan, h2c, row_w, wg, wu, wd)


def _route_plan(idx, w, ntok):
    tile = EXPERT_TILE
    n_ent = TOP_K * ntok
    p = n_ent + N_EXPERTS * tile
    flat_e = idx.reshape(-1)
    ent = jnp.arange(n_ent, dtype=jnp.int32)
    order = jnp.sort(flat_e * n_ent + ent) % n_ent
    counts = jnp.sum((flat_e[:, None] == jnp.arange(N_EXPERTS)[None, :]).astype(jnp.int32), axis=0)
    padded = ((counts + tile - 1) // tile) * tile
    gstart = jnp.cumsum(padded) - padded
    gend = gstart + padded
    ustart = jnp.cumsum(counts) - counts
    tstart = jnp.arange(p // tile, dtype=jnp.int32) * tile
    texp = jnp.sum((tstart[:, None] >= gend[None, :]).astype(jnp.int32), axis=1)
    used = tstart < gend[-1]
    texp = jnp.where(used, texp, N_EXPERTS - 1).astype(jnp.int32)
    row = jnp.arange(p, dtype=jnp.int32)
    row_e = jnp.repeat(texp, tile)
    local = row - gstart[row_e]
    real = jnp.repeat(used, tile) & (local < counts[row_e])
    src = order[jnp.clip(ustart[row_e] + local, 0, n_ent - 1)]
    pad_id = n_ent + jnp.cumsum(1 - real.astype(jnp.int32)) - 1
    entries = jnp.where(real, src, pad_id).astype(jnp.int32)
    row_w = jnp.where(real, w.reshape(-1)[src], 0.0).reshape(p, 1)
    return texp, entries, row_w


def _combine_kernel(x_ref, y0_ref, y1_ref, mb_ref, mc_ref, o_ref, *, tile, tpb, lc):
    gate = _row_mod(pl.program_id(0), mb_ref, mc_ref, 5, tile, tpb, lc)
    for j in range(CHUNK_ROWS):
        sl = slice(j * LANES, (j + 1) * LANES)
        y = y0_ref[pl.ds(j, tile, stride=CHUNK_ROWS), :] + y1_ref[pl.ds(j, tile, stride=CHUNK_ROWS), :]
        o_ref[:, sl] = x_ref[:, sl] + gate[:, sl] * y


def _combine(xn, ybuf, mods, nb, s, lc):
    ntok, d = xn.shape
    tile = s // 8
    tpb = s // tile
    nt = ntok // tile
    kern = functools.partial(_combine_kernel, tile=tile, tpb=tpb, lc=lc)
    return pl.pallas_call(
        kern,
        grid=(nt,),
        in_specs=[pl.BlockSpec((tile, d), lambda i: (i, 0)),
                  pl.BlockSpec((tile * CHUNK_ROWS, LANES), lambda i: (i, 0)),
                  pl.BlockSpec((tile * CHUNK_ROWS, LANES), lambda i: (nt + i, 0)),
                  pl.BlockSpec((1, 6, d), lambda i: (i // tpb, 0, 0)),
                  pl.BlockSpec((1, 6, d), lambda i: (nb, 0, 0))],
        out_specs=pl.BlockSpec((tile, d), lambda i: (i, 0)),
        out_shape=jax.ShapeDtypeStruct((ntok, d), F32),
        compiler_params=_cparams(("parallel",)),
        name="combine",
    )(xn, ybuf, ybuf, mods, mods)


def _rope_tables(l, lc, rot_dim, lane_off, group):
    n_freq = rot_dim // 4
    inv = jnp.asarray(ROPE_BASE ** (-np.arange(n_freq, dtype=np.float32) / n_freq), dtype=F32)
    t = jnp.arange(l, dtype=jnp.int32)
    row = (t // GRID_W).astype(F32)
    col = (t % GRID_W).astype(F32)
    ang = jnp.concatenate([row[:, None] * inv, col[:, None] * inv], axis=-1)
    cos, sin = jnp.cos(ang), jnp.sin(ang)
    cos_g = jnp.ones((l, group), F32).at[:, lane_off:lane_off + rot_dim].set(jnp.concatenate([cos, cos], -1))
    sin_g = jnp.zeros((l, group), F32).at[:, lane_off:lane_off + rot_dim].set(jnp.concatenate([-sin, sin], -1))
    reps = LANES // group
    cos_t = jnp.tile(cos_g, (1, reps))
    sin_t = jnp.tile(sin_g, (1, reps))
    cos_t = jnp.concatenate([jnp.ones((lc, LANES), F32), cos_t], axis=0)
    sin_t = jnp.concatenate([jnp.zeros((lc, LANES), F32), sin_t], axis=0)
    return cos_t, sin_t


def _na_bias(rpb, rows):
    nq = rows // NA_QROWS
    qi = np.arange(Q_TILE)
    ki = np.arange(NA_KEYS)
    qr_rel, qc = qi // GRID_W, qi % GRID_W
    kr_rel, kc = ki // GRID_W, ki % GRID_W
    win = np.clip(qc - NA_COLS // 2, 0, GRID_W - NA_COLS)
    cols = np.arange(GRID_W)
    dcol = cols[None, :] - cols[:, None] + NA_COLS - 1
    oh_c = (dcol[None] == np.arange(2 * NA_COLS - 1)[:, None, None]).astype(np.float32)
    by_col = jnp.einsum('hrd,dqk->hrqk', rpb.astype(F32), oh_c, precision=HIGHEST)
    tabs = []
    for t in (0, min(1, nq - 1), nq - 1):
        ks = int(np.clip(t * NA_QROWS - NA_ROWS // 2, 0, rows - NA_KROWS))
        qr = t * NA_QROWS + qr_rel
        kr = ks + kr_rel
        r0 = np.clip(qr - NA_ROWS // 2, 0, rows - NA_ROWS)
        ok = ((kr[None, :] >= r0[:, None]) & (kr[None, :] < r0[:, None] + NA_ROWS)
              & (kc[None, :] >= win[:, None]) & (kc[None, :] < win[:, None] + NA_COLS))
        drow = (ks + np.arange(NA_KROWS))[None, :] - (t * NA_QROWS + np.arange(NA_QROWS))[:, None] + NA_ROWS - 1
        oh_r = (drow[None] == np.arange(2 * NA_ROWS - 1)[:, None, None]).astype(np.float32)
        bias = jnp.einsum('rab,hrqk->haqbk', oh_r, by_col, precision=HIGHEST).reshape(-1, Q_TILE, NA_KEYS)
        tabs.append(jnp.where(jnp.asarray(ok)[None], bias, NEG_INF))
    tabs.append(jnp.full_like(tabs[0], NEG_INF))
    return jnp.stack(tabs)


def _layer_weights(w_in, b_w_q_up, b_w_kv_up, w_branch):
    cuts = np.cumsum([512, 128, 128, 512, 256, 32, 512, 512, 512, 512, 512, 512, 512, 512])
    (a_q, a_k, a_v, b_cq, b_ckv, b_kr, c_q, c_i, c_ff, c_fb, c_g, d_q, d_k, d_v, gates) = jnp.split(w_in, cuts, axis=1)
    d = w_in.shape[0]
    aq = a_q.reshape(d, A_HEADS, HEAD_DIM)
    aq = jnp.stack([aq[:, 0:4], aq[:, 4:8]], axis=2).reshape(d, A_HEADS * HEAD_DIM)
    w = jnp.concatenate([gates, aq, a_k, a_v, b_ckv, b_cq, c_q, c_i, c_ff, c_fb, c_g, d_q, d_k, d_v], axis=1)
    wkr = jnp.zeros((d, LANES), F32).at[:, B_NOPE:B_NOPE + B_ROPE].set(b_kr)
    wq = b_w_q_up.reshape(B_Q_RANK, B_HEADS, B_NOPE + B_ROPE)
    wq = jnp.pad(wq, ((0, 0), (0, 0), (0, LANES - B_NOPE - B_ROPE))).reshape(B_Q_RANK, B_HEADS * LANES)
    wkv = b_w_kv_up.reshape(B_KV_RANK, B_HEADS, B_NOPE + B_V)
    wk = jnp.pad(wkv[:, :, :B_NOPE], ((0, 0), (0, 0), (0, LANES - B_NOPE))).reshape(B_KV_RANK, B_HEADS * LANES)
    wv = wkv[:, :, B_NOPE:].reshape(B_KV_RANK, B_HEADS * B_V)
    wb0 = w_branch[0].reshape(A_HEADS, HEAD_DIM, -1)
    wb0 = jnp.stack([wb0[0:4], wb0[4:8]], axis=1).reshape(A_HEADS * HEAD_DIM, -1)
    wb = jnp.concatenate([wb0[None], w_branch[1:]], axis=0)
    cast = lambda t: t.astype(BF16)
    return cast(w), cast(wkr), cast(wq), cast(wk), cast(wv), cast(wb)


def _group_matrix(bounds):
    gid = np.zeros((LANES,), np.int32)
    size = np.zeros((LANES,), np.float32)
    for g, (lo, hi) in enumerate(bounds):
        gid[lo:hi] = g
        size[lo:hi] = hi - lo
    gmat = (gid[:, None] == gid[None, :]).astype(np.float32)
    return jnp.asarray(gmat, dtype=BF16), jnp.asarray(1.0 / size, dtype=F32).reshape(1, LANES)


def _pad_lanes(v, off=0):
    return jnp.zeros((1, LANES), F32).at[0, off:off + v.shape[0]].set(v.astype(F32))


def kernel(x, c, ctx, c_ctx, w_mod, b_mod, g_norm_mix, g_norm_ffn, w_in, a_gq, a_gk, a_sink, b_g_qa, b_g_kva,
           b_w_q_up, b_w_kv_up, b_gq, b_gk, c_lb_logits, c_g_out, d_gq, d_gk, d_rpb, w_branch, w_out,
           router_w, router_b, w_exp_gate, w_exp_up, w_exp_down):
    nb, l, d = x.shape
    lc = ctx.shape[1]
    s = lc + l
    ntok = nb * s
    depth = w_in.shape[0]
    assert l % Q_TILE == 0 and lc % Q_TILE == 0 and (l // GRID_W) >= NA_KROWS and s % 64 == 0

    xs = jnp.concatenate([ctx, x], axis=1).reshape(ntok, d)

    mod_rows = 16
    cvecs = jnp.zeros((mod_rows, d), F32).at[:nb].set(c).at[nb].set(c_ctx)
    mods = _adaln(cvecs, w_mod, b_mod).reshape(depth, mod_rows, 6, d)

    cos_a, sin_a = _rope_tables(l, lc, HEAD_DIM, 0, HEAD_DIM)
    cos_b, sin_b = _rope_tables(l, lc, B_ROPE, B_NOPE, LANES)
    gmat64, _ = _group_matrix([(0, 64), (64, 128)])
    gmat_b, inv_b = _group_matrix([(0, B_NOPE), (B_NOPE, B_NOPE + B_ROPE), (B_NOPE + B_ROPE, LANES)])
    lb_p = jax.nn.softmax(c_lb_logits.astype(F32), axis=1)
    lb_all = jnp.clip(jnp.cumsum(lb_p, axis=1) - lb_p[:, :1], 0.0, 1.0)
    rw_t = router_w.astype(F32).T
    rb = router_b.astype(F32).reshape(N_EXPERTS, 1)

    for lyr in range(depth):
        w, wkr, wq, wk, wv, wb = _layer_weights(w_in[lyr], b_w_q_up[lyr], b_w_kv_up[lyr], w_branch[lyr])
        m = mods[lyr]
        proj, kr = _in_proj(xs, m, g_norm_mix[lyr], w, wkr, nb, s, lc)

        two = lambda g: jnp.concatenate([g, g]).astype(F32)
        gvecs = jnp.stack([two(a_gq[lyr]), two(a_gk[lyr]), two(d_gq[lyr]), two(d_gk[lyr])])
        qk_ad = _prep_ad(proj, cos_a, sin_a, gvecs, gmat64, s)
        qb, kb, vb = _prep_b(proj, kr, wq, wk, wv, b_g_qa[lyr].reshape(1, -1), b_g_kva[lyr].reshape(1, -1),
                             _pad_lanes(b_gq[lyr]), _pad_lanes(b_gk[lyr]), gmat_b, inv_b, cos_b, sin_b, s)

        oa = _attn_a(a_sink[lyr].astype(F32), qk_ad, proj, nb, s, lc)
        ob = _attn_b(qb, kb, vb, nb, s, lc)
        ofw, obw = _gla(proj, lb_all[:, lyr], nb, s, lc)
        od = _attn_d(qk_ad, proj, _na_bias(d_rpb[lyr], l // GRID_W), nb, s, lc)

        y = _branch_mix(oa, ob, ofw, obw, od, proj, wb, c_g_out[lyr].reshape(1, -1).astype(F32), s)
        xn, h2c = _out_proj(y, xs, m, w_out[lyr].astype(BF16), g_norm_ffn[lyr].reshape(1, -1), nb, s, lc)

        idx, wts = _router(h2c, rw_t, rb)
        texp, entries, row_w = _route_plan(idx, wts, ntok)
        ybuf = _experts(h2c, texp, entries, row_w, w_exp_gate[lyr].astype(BF16),
                        w_exp_up[lyr].astype(BF16), w_exp_down[lyr].astype(BF16), ntok)
        xs = _combine(xn, ybuf, m, nb, s, lc)

    return xs.reshape(nb, s, d)[:, lc:]
```

```python
import functools

import numpy as np
import jax
import jax.numpy as jnp
from jax import lax
from jax.experimental import pallas as pl
from jax.experimental.pallas import tpu as pltpu

F32 = jnp.float32
BF16 = jnp.bfloat16
HIGHEST = lax.Precision.HIGHEST

GRID_W = 64
HEAD_DIM = 64
ROPE_BASE = 10000.0
EPS = 1e-6
F_MIN = 1e-6
NEG_INF = -1e30
A_HEADS, A_KV_HEADS, A_WINDOW = 8, 2, 128
B_HEADS, B_NOPE, B_ROPE, B_V, B_Q_RANK, B_KV_RANK = 8, 64, 32, 64, 512, 256
C_HEADS, C_DK, C_DV = 4, 128, 128
D_HEADS, NA_ROWS, NA_COLS = 8, 8, 16
N_BRANCH, BRANCH_W = 4, 512
N_EXPERTS, N_GROUPS, TOP_K, D_EXPERT = 16, 4, 2, 1024

LANES = 128
VMEM_LIMIT = 56 * 1024 * 1024

COL_GATES = 0
COL_AQ, COL_AK, COL_AV, COL_BCKV, COL_BCQ = 8192, 8704, 8832, 8960, 9216
COL_CQ, COL_CI, COL_CFF, COL_CFB, COL_CG = 9728, 10240, 10752, 11264, 11776
COL_DQ, COL_DK, COL_DV = 12288, 12800, 13312
PROJ_W = COL_DV + 512

Q_TILE = 256
GLA_CHUNK = 128
CHUNK_ROWS = 16
PACK_ROWS = 8
EXPERT_TILE = 512


def _cparams(sem, vmem=VMEM_LIMIT):
    return pltpu.CompilerParams(dimension_semantics=sem, vmem_limit_bytes=vmem)


def _sigmoid(x):
    return 1.0 / (1.0 + jnp.exp(-x))


def _silu(x):
    return x * _sigmoid(x)


def _gate_sigmoid(x):
    return 0.5 * jnp.tanh(0.5 * x) + 0.5


def _gate_silu(x):
    return x * _gate_sigmoid(x)


def _store_packed(ref, x, tile):
    for j in range(PACK_ROWS):
        lo = x[:, j * LANES:(j + 1) * LANES].astype(BF16).astype(F32)
        hi = x[:, (j + PACK_ROWS) * LANES:(j + PACK_ROWS + 1) * LANES].astype(BF16).astype(F32)
        word = ((pltpu.bitcast(hi, jnp.uint32) & jnp.uint32(0xFFFF0000))
                | lax.shift_right_logical(pltpu.bitcast(lo, jnp.uint32), jnp.uint32(16)))
        ref[pl.ds(j, tile, stride=PACK_ROWS), :] = word


def _load_packed(ref, j, tile):
    word = ref[pl.ds(j, tile, stride=PACK_ROWS), :]
    lo = pltpu.bitcast(lax.shift_left(word, jnp.uint32(16)), F32)
    hi = pltpu.bitcast(word & jnp.uint32(0xFFFF0000), F32)
    return lo, hi


def _rot_pairs(x, half):
    n = x.shape[-1]
    ax = x.ndim - 1
    lane = lax.broadcasted_iota(jnp.int32, x.shape, ax)
    fwd = pltpu.roll(x, n - half, axis=ax)
    bwd = pltpu.roll(x, half, axis=ax)
    return jnp.where((lane & (2 * half - 1)) < half, fwd, bwd)


def _row_mod(i, mb_ref, mc_ref, idx, tile, tiles_per_batch, lc):
    row = (i % tiles_per_batch) * tile + lax.broadcasted_iota(jnp.int32, (tile, 1), 0)
    return jnp.where(row < lc, mc_ref[0, idx:idx + 1, :], mb_ref[0, idx:idx + 1, :])


def _mod_norm(x, g, shift, scale):
    ms = jnp.mean(x * x, axis=-1, keepdims=True)
    return x * lax.rsqrt(ms + EPS) * g * (1.0 + scale) + shift


def _adaln_kernel(c_ref, w_ref, b_ref, o_ref):
    cv = c_ref[...]
    o_ref[0] = jnp.dot(_silu(cv), w_ref[0], preferred_element_type=F32, precision=HIGHEST) + b_ref[0]


def _adaln(cvecs, w_mod, b_mod):
    depth, d, n6 = w_mod.shape
    rows = cvecs.shape[0]
    tn = 1024
    return pl.pallas_call(
        _adaln_kernel,
        grid=(depth, n6 // tn),
        in_specs=[pl.BlockSpec((rows, d), lambda l, j: (0, 0)),
                  pl.BlockSpec((1, d, tn), lambda l, j: (l, 0, j)),
                  pl.BlockSpec((1, 1, tn), lambda l, j: (l, 0, j))],
        out_specs=pl.BlockSpec((1, rows, tn), lambda l, j: (l, 0, j)),
        out_shape=jax.ShapeDtypeStruct((depth, rows, n6), F32),
        compiler_params=_cparams(("parallel", "parallel")),
        name="adaln",
    )(cvecs, w_mod, b_mod.reshape(depth, 1, n6))


def _inproj_kernel(x_ref, mb_ref, mc_ref, g_ref, w_ref, wkr_ref, o_ref, kr_ref, h_scr, *, tile, tpb, lc):
    i = pl.program_id(0)

    @pl.when(pl.program_id(1) == 0)
    def _():
        shift = _row_mod(i, mb_ref, mc_ref, 0, tile, tpb, lc)
        scale = _row_mod(i, mb_ref, mc_ref, 1, tile, tpb, lc)
        hb = _mod_norm(x_ref[...], g_ref[...], shift, scale).astype(BF16)
        h_scr[...] = hb
        kr_ref[...] = jnp.dot(hb, wkr_ref[...], preferred_element_type=F32).astype(BF16)

    o_ref[...] = jnp.dot(h_scr[...], w_ref[...], preferred_element_type=F32).astype(BF16)


def _in_proj(xs, mods, g, w, wkr, nb, s, lc):
    ntok, d = xs.shape
    tile = s // 4
    tpb = s // tile
    tn = 1536
    kern = functools.partial(_inproj_kernel, tile=tile, tpb=tpb, lc=lc)
    return pl.pallas_call(
        kern,
        grid=(ntok // tile, PROJ_W // tn),
        in_specs=[pl.BlockSpec((tile, d), lambda i, j: (i, 0)),
                  pl.BlockSpec((1, 6, d), lambda i, j: (i // tpb, 0, 0)),
                  pl.BlockSpec((1, 6, d), lambda i, j: (nb, 0, 0)),
                  pl.BlockSpec((1, d), lambda i, j: (0, 0)),
                  pl.BlockSpec((d, tn), lambda i, j: (0, j)),
                  pl.BlockSpec((d, LANES), lambda i, j: (0, 0))],
        out_specs=[pl.BlockSpec((tile, tn), lambda i, j: (i, j)),
                   pl.BlockSpec((tile, LANES), lambda i, j: (i, 0))],
        out_shape=[jax.ShapeDtypeStruct((ntok, PROJ_W), BF16),
                   jax.ShapeDtypeStruct((ntok, LANES), BF16)],
        scratch_shapes=[pltpu.VMEM((tile, d), BF16)],
        compiler_params=_cparams(("parallel", "arbitrary")),
        name="in_proj",
    )(xs, mods, mods, g.reshape(1, d), w, wkr)


def _group_norm(x, gmat, inv_size, gvec):
    ss = jnp.dot((x * x).astype(BF16), gmat, preferred_element_type=F32)
    return x * lax.rsqrt(ss * inv_size + EPS) * gvec


def _prep_ad_kernel(aq_ref, ak_ref, d_ref, cos_ref, sin_ref, g_ref, gmat_ref, o_ref):
    gmat = gmat_ref[...]
    cos = cos_ref[...]
    sin = sin_ref[...]
    scale = HEAD_DIM ** -0.5
    inv = 1.0 / HEAD_DIM
    for blk in range(5):
        x = (aq_ref[:, blk * LANES:(blk + 1) * LANES] if blk < 4 else ak_ref[...]).astype(F32)
        gi = 0 if blk < 4 else 1
        xn = _group_norm(x, gmat, inv, g_ref[gi:gi + 1, :])
        xr = xn * cos + _rot_pairs(xn, HEAD_DIM // 2) * sin
        if blk < 4:
            xr = xr * scale
        ob = blk if blk < 4 else 12
        o_ref[:, ob * LANES:(ob + 1) * LANES] = xr.astype(BF16)
    for blk in range(8):
        x = d_ref[:, blk * LANES:(blk + 1) * LANES].astype(F32)
        gi = 2 if blk < 4 else 3
        xn = _group_norm(x, gmat, inv, g_ref[gi:gi + 1, :])
        if blk < 4:
            xn = xn * scale
        o_ref[:, (4 + blk) * LANES:(5 + blk) * LANES] = xn.astype(BF16)


def _prep_ad(proj, cos, sin, gvecs, gmat, s):
    ntok = proj.shape[0]
    tile = s // 4
    tpb = s // tile
    wa, wd = 4 * LANES, 8 * LANES
    return pl.pallas_call(
        _prep_ad_kernel,
        grid=(ntok // tile,),
        in_specs=[pl.BlockSpec((tile, wa), lambda i: (i, COL_AQ // wa)),
                  pl.BlockSpec((tile, LANES), lambda i: (i, COL_AK // LANES)),
                  pl.BlockSpec((tile, wd), lambda i: (i, COL_DQ // wd)),
                  pl.BlockSpec((tile, LANES), lambda i: (i % tpb, 0)),
                  pl.BlockSpec((tile, LANES), lambda i: (i % tpb, 0)),
                  pl.BlockSpec((4, LANES), lambda i: (0, 0)),
                  pl.BlockSpec((LANES, LANES), lambda i: (0, 0))],
        out_specs=pl.BlockSpec((tile, 13 * LANES), lambda i: (i, 0)),
        out_shape=jax.ShapeDtypeStruct((ntok, 13 * LANES), BF16),
        compiler_params=_cparams(("parallel",)),
        name="prep_ad",
    )(proj, proj, proj, cos, sin, gvecs, gmat)


def _prep_b_kernel(cq_ref, ckv_ref, kr_ref, wq_ref, wk_ref, wv_ref, gqa_ref, gkva_ref, gq_ref, gk_ref,
                   gmat_ref, inv_ref, cos_ref, sin_ref, q_ref, k_ref, v_ref):
    gmat = gmat_ref[...]
    inv = inv_ref[...]
    cos = cos_ref[...]
    sin = sin_ref[...]
    scale = (B_NOPE + B_ROPE) ** -0.5

    def rms(x, g):
        return x * lax.rsqrt(jnp.mean(x * x, axis=-1, keepdims=True) + EPS) * g

    cq = rms(cq_ref[...].astype(F32), gqa_ref[...]).astype(BF16)
    ckv = rms(ckv_ref[...].astype(F32), gkva_ref[...]).astype(BF16)
    q = jnp.dot(cq, wq_ref[...], preferred_element_type=F32)
    kn = jnp.dot(ckv, wk_ref[...], preferred_element_type=F32)
    v_ref[...] = jnp.dot(ckv, wv_ref[...], preferred_element_type=F32).astype(BF16)
    kro = _group_norm(kr_ref[...].astype(F32), gmat, inv, gk_ref[...])
    kro = kro * cos + _rot_pairs(kro, B_ROPE // 2) * sin
    for h in range(B_HEADS):
        sl = slice(h * LANES, (h + 1) * LANES)
        qh = _group_norm(q[:, sl], gmat, inv, gq_ref[...])
        qh = (qh * cos + _rot_pairs(qh, B_ROPE // 2) * sin) * scale
        q_ref[:, sl] = qh.astype(BF16)
        kh = _group_norm(kn[:, sl], gmat, inv, gk_ref[...]) + kro
        k_ref[:, sl] = kh.astype(BF16)


def _prep_b(proj, kr, wq, wk, wv, gqa, gkva, gq, gk, gmat, inv, cos, sin, s):
    ntok = proj.shape[0]
    tile = s // 4
    tpb = s // tile
    hw = B_HEADS * LANES
    full = lambda shape: pl.BlockSpec(shape, lambda i: (0, 0))
    return pl.pallas_call(
        _prep_b_kernel,
        grid=(ntok // tile,),
        in_specs=[pl.BlockSpec((tile, B_Q_RANK), lambda i: (i, COL_BCQ // B_Q_RANK)),
                  pl.BlockSpec((tile, B_KV_RANK), lambda i: (i, COL_BCKV // B_KV_RANK)),
                  pl.BlockSpec((tile, LANES), lambda i: (i, 0)),
                  full((B_Q_RANK, hw)), full((B_KV_RANK, hw)), full((B_KV_RANK, B_HEADS * B_V)),
                  full((1, B_Q_RANK)), full((1, B_KV_RANK)), full((1, LANES)), full((1, LANES)),
                  full((LANES, LANES)), full((1, LANES)),
                  pl.BlockSpec((tile, LANES), lambda i: (i % tpb, 0)),
                  pl.BlockSpec((tile, LANES), lambda i: (i % tpb, 0))],
        out_specs=[pl.BlockSpec((tile, hw), lambda i: (i, 0)),
                   pl.BlockSpec((tile, hw), lambda i: (i, 0)),
                   pl.BlockSpec((tile, B_HEADS * B_V), lambda i: (i, 0))],
        out_shape=[jax.ShapeDtypeStruct((ntok, hw), BF16),
                   jax.ShapeDtypeStruct((ntok, hw), BF16),
                   jax.ShapeDtypeStruct((ntok, B_HEADS * B_V), BF16)],
        compiler_params=_cparams(("parallel",)),
        name="prep_b",
    )(proj, proj, kr, wq, wk, wv, gqa, gkva, gq, gk, gmat, inv, cos, sin)


def _nt_dot(a, b):
    return lax.dot_general(a, b, (((1,), (1,)), ((), ())), preferred_element_type=F32)


def _half_mask(shape, half):
    lane = lax.broadcasted_iota(jnp.int32, shape, len(shape) - 1)
    return (lane >= HEAD_DIM) if half else (lane < HEAD_DIM)


def _joint_attend(s, sc, vw, vc, sink=None):
    m = jnp.maximum(jnp.max(s, axis=-1, keepdims=True), jnp.max(sc, axis=-1, keepdims=True))
    if sink is not None:
        m = jnp.maximum(m, sink)
    p = jnp.exp(s - m)
    pc = jnp.exp(sc - m)
    den = jnp.sum(p, axis=-1, keepdims=True) + jnp.sum(pc, axis=-1, keepdims=True)
    if sink is not None:
        den = den + jnp.exp(sink - m)
    o = (jnp.dot(p.astype(BF16), vw, preferred_element_type=F32)
         + jnp.dot(pc.astype(BF16), vc, preferred_element_type=F32))
    return o / den


def _attn_a_kernel(sink_ref, q_ref, k_ref, v_ref, o_ref, *, lc, l):
    qt = pl.program_id(1)
    span = Q_TILE + 2 * A_WINDOW
    qstart = (qt - 1) * Q_TILE
    ws = jnp.clip(qstart - A_WINDOW, 0, l - span)
    wrow = pl.multiple_of(lc + ws, LANES)
    kw = k_ref[pl.ds(wrow, span), :]
    vw = v_ref[pl.ds(wrow, span), :]
    kc = k_ref[0:lc, :]
    vc = v_ref[0:lc, :]
    kpos = ws + lax.broadcasted_iota(jnp.int32, (Q_TILE, span), 1)
    qoff = jnp.where(qt > 0, qstart, -(1 << 20))
    qpos = qoff + lax.broadcasted_iota(jnp.int32, (Q_TILE, span), 0)
    ok = jnp.abs(kpos - qpos) <= A_WINDOW
    for blk in range(4):
        qb = q_ref[:, blk * LANES:(blk + 1) * LANES]
        halves = []
        for half in range(2):
            head = blk + 4 * half
            qm = jnp.where(_half_mask(qb.shape, half), qb, jnp.zeros_like(qb))
            s = jnp.where(ok, _nt_dot(qm, kw), NEG_INF)
            sc = _nt_dot(qm, kc)
            halves.append(_joint_attend(s, sc, vw, vc, sink_ref[head]))
        out = jnp.where(_half_mask(halves[0].shape, 0), halves[0], halves[1])
        o_ref[:, blk * LANES:(blk + 1) * LANES] = out.astype(BF16)


def _attn_a(sink, qk, proj, nb, s, lc):
    ntok = qk.shape[0]
    nq = s // Q_TILE
    kern = functools.partial(_attn_a_kernel, lc=lc, l=s - lc)
    return pl.pallas_call(
        kern,
        grid_spec=pltpu.PrefetchScalarGridSpec(
            num_scalar_prefetch=0,
            grid=(nb, nq),
            in_specs=[pl.BlockSpec(memory_space=pltpu.SMEM),
                      pl.BlockSpec((Q_TILE, 4 * LANES), lambda b, t: (b * nq + t, 0)),
                      pl.BlockSpec((s, LANES), lambda b, t: (b, 12)),
                      pl.BlockSpec((s, LANES), lambda b, t: (b, COL_AV // LANES))],
            out_specs=pl.BlockSpec((Q_TILE, 4 * LANES), lambda b, t: (b * nq + t, 0)),
        ),
        out_shape=jax.ShapeDtypeStruct((ntok, 4 * LANES), BF16),
        compiler_params=_cparams(("parallel", "arbitrary")),
        name="attn_a",
    )(sink, qk, qk, proj)


def _attn_b_kernel(q_ref, k_ref, v_ref, o_ref, s_scr, p_scr, *, s, lc):
    qt = pl.program_id(2)

    def head(hh, nkeys):
        hl = slice(hh * LANES, (hh + 1) * LANES)
        q = q_ref[:, hl]
        for c in range(nkeys // Q_TILE):
            rows = slice(c * Q_TILE, (c + 1) * Q_TILE)
            s_scr[hh, :, rows] = _nt_dot(q, k_ref[rows, hl])
        mx = s_scr[hh, :, 0:LANES]
        for j in range(1, nkeys // LANES):
            mx = jnp.maximum(mx, s_scr[hh, :, j * LANES:(j + 1) * LANES])
        m = jnp.broadcast_to(jnp.max(mx, axis=-1, keepdims=True), (Q_TILE, LANES))
        lsum = jnp.zeros((Q_TILE, LANES), F32)
        for j in range(nkeys // LANES):
            cols = slice(j * LANES, (j + 1) * LANES)
            p = jnp.exp(s_scr[hh, :, cols] - m)
            lsum = lsum + p
            p_scr[hh, :, cols] = p.astype(BF16)
        den = jnp.sum(lsum, axis=-1, keepdims=True)
        o = jnp.dot(p_scr[hh, :, 0:nkeys], v_ref[0:nkeys, :], preferred_element_type=F32)
        return o / den

    def both(nkeys):
        r0 = head(0, nkeys)
        r1 = head(1, nkeys)
        o_ref[...] = jnp.where(_half_mask(r0.shape, 0), r0, r1).astype(BF16)

    @pl.when(qt == 0)
    def _():
        both(lc)

    @pl.when(qt > 0)
    def _():
        both(s)


def _attn_b(qb, kb, vb, nb, s, lc):
    ntok = qb.shape[0]
    nq = s // Q_TILE
    kern = functools.partial(_attn_b_kernel, s=s, lc=lc)
    return pl.pallas_call(
        kern,
        grid=(nb, B_HEADS // 2, nq),
        in_specs=[pl.BlockSpec((Q_TILE, 2 * LANES), lambda b, p, t: (b * nq + t, p)),
                  pl.BlockSpec((s, 2 * LANES), lambda b, p, t: (b, p)),
                  pl.BlockSpec((s, LANES), lambda b, p, t: (b, p))],
        out_specs=pl.BlockSpec((Q_TILE, LANES), lambda b, p, t: (b * nq + t, p)),
        out_shape=jax.ShapeDtypeStruct((ntok, B_HEADS * B_V), BF16),
        scratch_shapes=[pltpu.VMEM((2, Q_TILE, s), F32), pltpu.VMEM((2, Q_TILE, s), BF16)],
        compiler_params=_cparams(("parallel", "parallel", "arbitrary")),
        name="attn_b",
    )(qb, kb, vb)


NA_QROWS = Q_TILE // GRID_W
NA_KROWS = NA_QROWS + NA_ROWS
NA_KEYS = NA_KROWS * GRID_W


def _attn_d_kernel(q_ref, k_ref, v_ref, bias_ref, o_ref, *, lc, rows):
    qt = pl.program_id(1)
    ks = jnp.clip((qt - 1) * NA_QROWS - NA_ROWS // 2, 0, rows - NA_KROWS)
    wrow = pl.multiple_of(lc + ks * GRID_W, Q_TILE)
    for blk in range(4):
        sl = slice(blk * LANES, (blk + 1) * LANES)
        qb = q_ref[:, sl]
        kw = k_ref[pl.ds(wrow, NA_KEYS), sl]
        vw = v_ref[pl.ds(wrow, NA_KEYS), sl]
        kc = k_ref[0:lc, sl]
        vc = v_ref[0:lc, sl]
        halves = []
        for half in range(2):
            head = 2 * blk + half
            qm = jnp.where(_half_mask(qb.shape, half), qb, jnp.zeros_like(qb))
            s = _nt_dot(qm, kw) + bias_ref[0, head]
            sc = _nt_dot(qm, kc)
            halves.append(_joint_attend(s, sc, vw, vc))
        out = jnp.where(_half_mask(halves[0].shape, 0), halves[0], halves[1])
        o_ref[:, sl] = out.astype(BF16)


def _na_variant(t, nq):
    return jnp.where(t == 0, 3, jnp.where(t == 1, 0, jnp.where(t == nq - 1, 2, 1)))


def _attn_d(qk, proj, bias, nb, s, lc):
    ntok = qk.shape[0]
    nq = s // Q_TILE
    rows = (s - lc) // GRID_W
    kern = functools.partial(_attn_d_kernel, lc=lc, rows=rows)
    w = 4 * LANES
    return pl.pallas_call(
        kern,
        grid=(nb, nq),
        in_specs=[pl.BlockSpec((Q_TILE, w), lambda b, t: (b * nq + t, 1)),
                  pl.BlockSpec((s, w), lambda b, t: (b, 2)),
                  pl.BlockSpec((s, w), lambda b, t: (b, COL_DV // w)),
                  pl.BlockSpec((1, D_HEADS, Q_TILE, NA_KEYS), lambda b, t: (_na_variant(t, nq), 0, 0, 0))],
        out_specs=pl.BlockSpec((Q_TILE, w), lambda b, t: (b * nq + t, 0)),
        out_shape=jax.ShapeDtypeStruct((ntok, w), BF16),
        compiler_params=_cparams(("parallel", "arbitrary")),
        name="attn_d",
    )(qk, qk, proj, bias)


def _cumsum_rows(x, reverse):
    n = x.shape[0]
    row = lax.broadcasted_iota(jnp.int32, x.shape, 0)
    step = 1
    while step < n:
        if reverse:
            x = x + jnp.where(row < n - step, pltpu.roll(x, n - step, axis=0), 0.0)
        else:
            x = x + jnp.where(row >= step, pltpu.roll(x, step, axis=0), 0.0)
        step *= 2
    return x


def _block_ref(b, size, idx):
    n, w = b.shape
    b3 = b.reshape(n // size, size, w)
    return jnp.broadcast_to(b3[:, idx:idx + 1, :], b3.shape).reshape(n, w)


def _gla_chunk(q, v, fpre, lb, st_ref, reverse):
    c = q.shape[0]
    fg = lb + (1.0 - lb) * _sigmoid(fpre)
    logf = jnp.log(jnp.maximum(fg, F_MIN))
    k = 1.0 - fg
    qs = _silu(q)
    b = _cumsum_rows(logf, reverse)
    row = lax.broadcasted_iota(jnp.int32, (c, c), 0)
    col = lax.broadcasted_iota(jnp.int32, (c, c), 1)
    xor = row ^ col
    after = (row < col) if reverse else (row > col)

    base = 8
    bref = _block_ref(b, base, base // 2 if reverse else base // 2 - 1)
    qe = (qs * jnp.exp(b - bref)).astype(BF16)
    ke = (k * jnp.exp(bref - b)).astype(BF16)
    a = jnp.where((xor < base) & (after | (row == col)), _nt_dot(qe, ke), 0.0)
    m = base
    while m < c:
        bref = _block_ref(b, 2 * m, m if reverse else m - 1)
        qe = (qs * jnp.exp(jnp.minimum(b - bref, 0.0))).astype(BF16)
        ke = (k * jnp.exp(jnp.minimum(bref - b, 0.0))).astype(BF16)
        a = jnp.where((xor >= m) & (xor < 2 * m) & after, _nt_dot(qe, ke), a)
        m *= 2

    st = st_ref[...]
    vb = v.astype(BF16)
    o = (jnp.dot(a.astype(BF16), vb, preferred_element_type=F32)
         + _nt_dot((qs * jnp.exp(b)).astype(BF16), st.astype(BF16)))
    btot = b[0:1, :] if reverse else b[c - 1:c, :]
    kdec = (k * jnp.exp(btot - b)).astype(BF16)
    upd = lax.dot_general(vb, kdec, (((0,), (0,)), ((), ())), preferred_element_type=F32)
    st_ref[...] = st * jnp.exp(btot) + upd
    return o


def _gla_kernel(qf_ref, vf_ref, ff_ref, qb_ref, vb_ref, fb_ref, lb_ref, of_ref, ob_ref, stf, stb):
    @pl.when(pl.program_id(1) == 0)
    def _():
        stf[...] = jnp.zeros_like(stf)
        stb[...] = jnp.zeros_like(stb)

    for h in range(C_HEADS):
        sl = slice(h * LANES, (h + 1) * LANES)
        of_ref[:, sl] = _gla_chunk(qf_ref[:, sl].astype(F32), vf_ref[:, sl].astype(F32),
                                   ff_ref[:, sl].astype(F32), lb_ref[0:1, sl], stf.at[h], False)
        ob_ref[:, sl] = _gla_chunk(qb_ref[:, sl].astype(F32), vb_ref[:, sl].astype(F32),
                                   fb_ref[:, sl].astype(F32), lb_ref[1:2, sl], stb.at[h], True)


def _gla(proj, lb, nb, s, lc):
    ntok = proj.shape[0]
    c = GLA_CHUNK
    ns = s // c
    nctx = lc // c
    w = C_HEADS * C_DK

    def bw_chunk(t):
        return jnp.where(t < nctx, nctx - 1 - t, ns - 1 - (t - nctx))

    def fw(col):
        return pl.BlockSpec((c, w), lambda b, t: (b * ns + t, col // w))

    def bw(col):
        return pl.BlockSpec((c, w), lambda b, t: (b * ns + bw_chunk(t), col // w))

    return pl.pallas_call(
        _gla_kernel,
        grid=(nb, ns),
        in_specs=[fw(COL_CQ), fw(COL_CI), fw(COL_CFF), bw(COL_CQ), bw(COL_CI), bw(COL_CFB),
                  pl.BlockSpec((2, w), lambda b, t: (0, 0))],
        out_specs=[pl.BlockSpec((c, w), lambda b, t: (b * ns + t, 0)),
                   pl.BlockSpec((c, w), lambda b, t: (b * ns + bw_chunk(t), 0))],
        out_shape=[jax.ShapeDtypeStruct((ntok, C_HEADS * C_DV), F32)] * 2,
        scratch_shapes=[pltpu.VMEM((C_HEADS, C_DV, C_DK), F32), pltpu.VMEM((C_HEADS, C_DV, C_DK), F32)],
        compiler_params=_cparams(("parallel", "arbitrary")),
        name="gla",
    )(proj, proj, proj, proj, proj, proj, lb)


def _mix_kernel(oa_ref, ob_ref, ofw_ref, obw_ref, cg_ref, od_ref, g0_ref, g1_ref, g2_ref, g3_ref,
                wb_ref, gout_ref, y_ref, oc_scr):
    @pl.when(pl.program_id(1) == 0)
    def _():
        gout = gout_ref[...]
        for h in range(C_HEADS):
            sl = slice(h * LANES, (h + 1) * LANES)
            o = ofw_ref[:, sl] + obw_ref[:, sl]
            on = o * lax.rsqrt(jnp.mean(o * o, axis=-1, keepdims=True) + EPS) * gout
            oc_scr[:, sl] = (on * _gate_silu(cg_ref[:, sl].astype(F32))).astype(BF16)

    outs = (oa_ref[...], ob_ref[...], oc_scr[...], od_ref[...])
    gates = (g0_ref, g1_ref, g2_ref, g3_ref)
    y = None
    for br in range(N_BRANCH):
        t = _gate_sigmoid(gates[br][...].astype(F32)) * jnp.dot(outs[br], wb_ref[br], preferred_element_type=F32)
        y = t if y is None else y + t
    y_ref[...] = y.astype(BF16)


def _branch_mix(oa, ob, ofw, obw, od, proj, wb, gout, s):
    ntok = oa.shape[0]
    d = wb.shape[2]
    tile = s // 8
    tn = 1024
    row512 = pl.BlockSpec((tile, 512), lambda i, n: (i, 0))

    def gate(br):
        return pl.BlockSpec((tile, tn), lambda i, n: (i, (COL_GATES + br * d) // tn + n))

    return pl.pallas_call(
        _mix_kernel,
        grid=(ntok // tile, d // tn),
        in_specs=[row512, row512, row512, row512,
                  pl.BlockSpec((tile, 512), lambda i, n: (i, COL_CG // 512)),
                  row512, gate(0), gate(1), gate(2), gate(3),
                  pl.BlockSpec((N_BRANCH, BRANCH_W, tn), lambda i, n: (0, 0, n)),
                  pl.BlockSpec((1, LANES), lambda i, n: (0, 0))],
        out_specs=pl.BlockSpec((tile, tn), lambda i, n: (i, n)),
        out_shape=jax.ShapeDtypeStruct((ntok, d), BF16),
        scratch_shapes=[pltpu.VMEM((tile, 512), BF16)],
        compiler_params=_cparams(("parallel", "arbitrary")),
        name="branch_mix",
    )(oa, ob, ofw, obw, proj, od, proj, proj, proj, proj, wb, gout)


def _outproj_kernel(y_ref, x_ref, mb_ref, mc_ref, wo_ref, gffn_ref, xn_ref, h2_ref, h2p_ref, *, tile, tpb, lc):
    i = pl.program_id(0)
    out = jnp.dot(y_ref[...], wo_ref[...], preferred_element_type=F32)
    xn = x_ref[...] + _row_mod(i, mb_ref, mc_ref, 2, tile, tpb, lc) * out
    xn_ref[...] = xn
    h2 = _mod_norm(xn, gffn_ref[...], _row_mod(i, mb_ref, mc_ref, 3, tile, tpb, lc),
                   _row_mod(i, mb_ref, mc_ref, 4, tile, tpb, lc))
    for j in range(CHUNK_ROWS):
        h2_ref[pl.ds(j, tile, stride=CHUNK_ROWS), :] = h2[:, j * LANES:(j + 1) * LANES]
    _store_packed(h2p_ref, h2, tile)


def _out_proj(y, xs, mods, wo, gffn, nb, s, lc):
    ntok, d = xs.shape
    tile = s // 8
    tpb = s // tile
    kern = functools.partial(_outproj_kernel, tile=tile, tpb=tpb, lc=lc)
    return pl.pallas_call(
        kern,
        grid=(ntok // tile,),
        in_specs=[pl.BlockSpec((tile, d), lambda i: (i, 0)),
                  pl.BlockSpec((tile, d), lambda i: (i, 0)),
                  pl.BlockSpec((1, 6, d), lambda i: (i // tpb, 0, 0)),
                  pl.BlockSpec((1, 6, d), lambda i: (nb, 0, 0)),
                  pl.BlockSpec((d, d), lambda i: (0, 0), pipeline_mode=pl.Buffered(1)),
                  pl.BlockSpec((1, d), lambda i: (0, 0))],
        out_specs=[pl.BlockSpec((tile, d), lambda i: (i, 0)),
                   pl.BlockSpec((tile * CHUNK_ROWS, LANES), lambda i: (i, 0)),
                   pl.BlockSpec((tile * PACK_ROWS, LANES), lambda i: (i, 0))],
        out_shape=[jax.ShapeDtypeStruct((ntok, d), F32),
                   jax.ShapeDtypeStruct((ntok * CHUNK_ROWS, LANES), F32),
                   jax.ShapeDtypeStruct((ntok * PACK_ROWS, LANES), jnp.uint32)],
        compiler_params=_cparams(("parallel",)),
        name="out_proj",
    )(y, xs, mods, mods, wo, gffn)


def _router_kernel(h_ref, rw_ref, rb_ref, idx_ref, w_ref, *, tile):
    logits = None
    for j in range(CHUNK_ROWS):
        hj = h_ref[pl.ds(j, tile, stride=CHUNK_ROWS), :]
        part = lax.dot_general(rw_ref[:, j * LANES:(j + 1) * LANES], hj, (((1,), (1,)), ((), ())),
                               preferred_element_type=F32, precision=HIGHEST)
        logits = part if logits is None else logits + part
    scores = _sigmoid(logits)
    biased = scores + rb_ref[...]
    sc = [scores[e:e + 1, :] for e in range(N_EXPERTS)]
    bi = [biased[e:e + 1, :] for e in range(N_EXPERTS)]
    per = N_EXPERTS // N_GROUPS
    best_g = jnp.zeros((1, tile), jnp.int32)
    best_s = None
    for g in range(N_GROUPS):
        members = bi[g * per:(g + 1) * per]
        gs = None
        for a in range(per):
            for b in range(a + 1, per):
                pair = members[a] + members[b]
                gs = pair if gs is None else jnp.maximum(gs, pair)
        if best_s is None:
            best_s = gs
        else:
            better = gs > best_s
            best_g = jnp.where(better, g, best_g)
            best_s = jnp.where(better, gs, best_s)
    picks = []
    taken = jnp.full((1, tile), -1, jnp.int32)
    for _ in range(TOP_K):
        bv = jnp.full((1, tile), -3e38, F32)
        be = jnp.zeros((1, tile), jnp.int32)
        bw = jnp.zeros((1, tile), F32)
        for e in range(N_EXPERTS):
            val = jnp.where(best_g == e // per, bi[e], NEG_INF)
            better = (val > bv) & (taken != e)
            bv = jnp.where(better, val, bv)
            be = jnp.where(better, e, be)
            bw = jnp.where(better, sc[e], bw)
        picks.append((be, bw))
        taken = be
    tot = picks[0][1] + picks[1][1]
    idx_ref[...] = jnp.concatenate([picks[0][0], picks[1][0]], axis=0)
    w_ref[...] = jnp.concatenate([picks[0][1] / tot, picks[1][1] / tot], axis=0)


def _router(h2c, rw_t, rb):
    ntok = h2c.shape[0] // CHUNK_ROWS
    tile = 1024 if ntok % 1024 == 0 else 256
    d = rw_t.shape[1]
    kern = functools.partial(_router_kernel, tile=tile)
    return pl.pallas_call(
        kern,
        grid=(ntok // tile,),
        in_specs=[pl.BlockSpec((tile * CHUNK_ROWS, LANES), lambda i: (i, 0)),
                  pl.BlockSpec((N_EXPERTS, d), lambda i: (0, 0)),
                  pl.BlockSpec((N_EXPERTS, 1), lambda i: (0, 0))],
        out_specs=[pl.BlockSpec((TOP_K, tile), lambda i: (0, i)),
                   pl.BlockSpec((TOP_K, tile), lambda i: (0, i))],
        out_shape=[jax.ShapeDtypeStruct((TOP_K, ntok), jnp.int32),
                   jax.ShapeDtypeStruct((TOP_K, ntok), F32)],
        compiler_params=_cparams(("parallel",)),
        name="router",
    )(h2c, rw_t, rb)


def _expert_kernel(texp_ref, ent_ref, h_hbm, rw_ref, wg_ref, wu_ref, wd_ref, y_hbm,
                   gbuf0, gbuf1, xbuf, obuf0, obuf1, gsem, ssem, *, ntok, tile):
    i = pl.program_id(0)
    n = pl.num_programs(0)
    slab = tile * PACK_ROWS
    gbufs = (gbuf0, gbuf1)
    obufs = (obuf0, obuf1)

    def rows(r):
        return pl.ds(pl.multiple_of(r * PACK_ROWS, PACK_ROWS), PACK_ROWS)

    def gather_row(t, r, sl):
        e = ent_ref[(t + 1) * tile + r]
        src = jnp.where(e >= 2 * ntok, 0, jnp.where(e >= ntok, e - ntok, e))
        pltpu.make_async_copy(h_hbm.at[rows(src), :], gbufs[sl].at[rows(r), :], gsem.at[sl]).start()

    def scatter_row(t, r, sl):
        e = ent_ref[(t + 1) * tile + r]
        pltpu.make_async_copy(obufs[sl].at[rows(r), :], y_hbm.at[rows(e), :], ssem.at[sl]).start()

    def wait_gather(sl):
        pltpu.make_async_copy(h_hbm.at[pl.ds(0, slab), :], gbufs[sl], gsem.at[sl]).wait()

    def wait_scatter(sl):
        pltpu.make_async_copy(obufs[sl], y_hbm.at[pl.ds(0, slab), :], ssem.at[sl]).wait()

    def looped(fn):
        def body(r, carry):
            fn(r)
            return carry
        lax.fori_loop(0, tile, body, 0, unroll=8)

    def step(sl):
        other = 1 - sl

        @pl.when(i == 0)
        def _():
            obufs[other][...] = jnp.zeros((slab, LANES), jnp.uint32)
            looped(lambda r: gather_row(0, r, sl))

        wait_gather(sl)

        def issue(part):
            lo, hi = part * tile // 4, (part + 1) * tile // 4
            for r in range(lo, hi):
                gather_row(i + 1, r, other)
                scatter_row(i - 1, r, other)

        issue(0)
        for j in range(PACK_ROWS):
            lo, hi = _load_packed(gbufs[sl], j, tile)
            xbuf[:, j * LANES:(j + 1) * LANES] = lo.astype(BF16)
            xbuf[:, (j + PACK_ROWS) * LANES:(j + PACK_ROWS + 1) * LANES] = hi.astype(BF16)
        x = xbuf[...]
        issue(1)
        hg = jnp.dot(x, wg_ref[0], preferred_element_type=F32)
        issue(2)
        hu = jnp.dot(x, wu_ref[0], preferred_element_type=F32)
        he = (_gate_silu(hg) * hu).astype(BF16)
        issue(3)
        y = jnp.dot(he, wd_ref[0], preferred_element_type=F32) * rw_ref[...]

        @pl.when(i >= 1)
        def _():
            wait_scatter(sl)

        _store_packed(obufs[sl], y, tile)

        @pl.when(i == n - 1)
        def _():
            looped(lambda r: scatter_row(i, r, sl))
            wait_scatter(other)
            wait_scatter(sl)
            wait_gather(other)

    @pl.when(i % 2 == 0)
    def _():
        step(0)

    @pl.when(i % 2 == 1)
    def _():
        step(1)


def _experts(h2p, tile_exp, entries, row_w, wg, wu, wd, ntok):
    tile = EXPERT_TILE
    p = entries.shape[0]
    d = wg.shape[1]
    de = wg.shape[2]
    lead = p + jnp.arange(tile, dtype=jnp.int32)
    trail = jnp.full((tile,), TOP_K * ntok, jnp.int32)
    plan = jnp.concatenate([lead, entries, trail])
    slab = pltpu.VMEM((tile * PACK_ROWS, LANES), jnp.uint32)
    kern = functools.partial(_expert_kernel, ntok=ntok, tile=tile)
    return pl.pallas_call(
        kern,
        grid_spec=pltpu.PrefetchScalarGridSpec(
            num_scalar_prefetch=2,
            grid=(p // tile,),
            in_specs=[pl.BlockSpec(memory_space=pl.ANY),
                      pl.BlockSpec((tile, 1), lambda i, te, en: (i, 0)),
                      pl.BlockSpec((1, d, de), lambda i, te, en: (te[i], 0, 0)),
                      pl.BlockSpec((1, d, de), lambda i, te, en: (te[i], 0, 0)),
                      pl.BlockSpec((1, de, d), lambda i, te, en: (te[i], 0, 0))],
            out_specs=pl.BlockSpec(memory_space=pl.ANY),
            scratch_shapes=[slab, slab, pltpu.VMEM((tile, d), BF16), slab, slab,
                            pltpu.SemaphoreType.DMA((2,)),
                            pltpu.SemaphoreType.DMA((2,))]),
        out_shape=jax.ShapeDtypeStruct(((p + tile) * PACK_ROWS, LANES), jnp.uint32),
        compiler_params=_cparams(("arbitrary",)),
        name="experts",
    )(tile_exp, plan, h2p, row_w, wg, wu, wd)


def _route_plan(idx, w, ntok):
    tile = EXPERT_TILE
    n_ent = TOP_K * ntok
    p = n_ent + N_EXPERTS * tile
    flat_e = idx.reshape(-1)
    ent = jnp.arange(n_ent, dtype=jnp.int32)
    order = jnp.sort(flat_e * n_ent + ent) % n_ent
    counts = jnp.sum((flat_e[None, :] == jnp.arange(N_EXPERTS)[:, None]).astype(jnp.int32), axis=1)
    padded = ((counts + tile - 1) // tile) * tile
    gstart = jnp.cumsum(padded) - padded
    gend = gstart + padded
    ustart = jnp.cumsum(counts) - counts
    tstart = jnp.arange(p // tile, dtype=jnp.int32) * tile
    texp = jnp.sum((tstart[:, None] >= gend[None, :]).astype(jnp.int32), axis=1)
    used = tstart < gend[-1]
    texp = jnp.where(used, texp, N_EXPERTS - 1).astype(jnp.int32)
    local = (tstart - gstart[texp])[:, None] + jnp.arange(tile, dtype=jnp.int32)[None, :]
    real = (used[:, None] & (local < counts[texp][:, None])).reshape(p)
    src = order[jnp.clip(ustart[texp][:, None] + local, 0, n_ent - 1).reshape(p)]
    pad_id = n_ent + jnp.cumsum(1 - real.astype(jnp.int32)) - 1
    entries = jnp.where(real, src, pad_id).astype(jnp.int32)
    row_w = jnp.where(real, w.reshape(-1)[src], 0.0).reshape(p, 1)
    return texp, entries, row_w


def _combine_kernel(x_ref, y0_ref, y1_ref, mb_ref, mc_ref, o_ref, *, tile, tpb, lc):
    gate = _row_mod(pl.program_id(0), mb_ref, mc_ref, 5, tile, tpb, lc)
    for j in range(PACK_ROWS):
        lo0, hi0 = _load_packed(y0_ref, j, tile)
        lo1, hi1 = _load_packed(y1_ref, j, tile)
        for jj, y in ((j, lo0 + lo1), (j + PACK_ROWS, hi0 + hi1)):
            sl = slice(jj * LANES, (jj + 1) * LANES)
            o_ref[:, sl] = x_ref[:, sl] + gate[:, sl] * y


def _combine(xn, ybuf, mods, nb, s, lc):
    ntok, d = xn.shape
    tile = s // 8
    tpb = s // tile
    nt = ntok // tile
    kern = functools.partial(_combine_kernel, tile=tile, tpb=tpb, lc=lc)
    return pl.pallas_call(
        kern,
        grid=(nt,),
        in_specs=[pl.BlockSpec((tile, d), lambda i: (i, 0)),
                  pl.BlockSpec((tile * PACK_ROWS, LANES), lambda i: (i, 0)),
                  pl.BlockSpec((tile * PACK_ROWS, LANES), lambda i: (nt + i, 0)),
                  pl.BlockSpec((1, 6, d), lambda i: (i // tpb, 0, 0)),
                  pl.BlockSpec((1, 6, d), lambda i: (nb, 0, 0))],
        out_specs=pl.BlockSpec((tile, d), lambda i: (i, 0)),
        out_shape=jax.ShapeDtypeStruct((ntok, d), F32),
        compiler_params=_cparams(("parallel",)),
        name="combine",
    )(xn, ybuf, ybuf, mods, mods)


def _rope_tables(l, lc, rot_dim, lane_off, group):
    n_freq = rot_dim // 4
    inv = jnp.asarray(ROPE_BASE ** (-np.arange(n_freq, dtype=np.float32) / n_freq), dtype=F32)
    t = jnp.arange(l, dtype=jnp.int32)
    row = (t // GRID_W).astype(F32)
    col = (t % GRID_W).astype(F32)
    ang = jnp.concatenate([row[:, None] * inv, col[:, None] * inv], axis=-1)
    cos, sin = jnp.cos(ang), jnp.sin(ang)
    cos_g = jnp.ones((l, group), F32).at[:, lane_off:lane_off + rot_dim].set(jnp.concatenate([cos, cos], -1))
    sin_g = jnp.zeros((l, group), F32).at[:, lane_off:lane_off + rot_dim].set(jnp.concatenate([-sin, sin], -1))
    reps = LANES // group
    cos_t = jnp.tile(cos_g, (1, reps))
    sin_t = jnp.tile(sin_g, (1, reps))
    cos_t = jnp.concatenate([jnp.ones((lc, LANES), F32), cos_t], axis=0)
    sin_t = jnp.concatenate([jnp.zeros((lc, LANES), F32), sin_t], axis=0)
    return cos_t, sin_t


def _na_bias(rpb, rows):
    nq = rows // NA_QROWS
    qi = np.arange(Q_TILE)
    ki = np.arange(NA_KEYS)
    qr_rel, qc = qi // GRID_W, qi % GRID_W
    kr_rel, kc = ki // GRID_W, ki % GRID_W
    win = np.clip(qc - NA_COLS // 2, 0, GRID_W - NA_COLS)
    cols = np.arange(GRID_W)
    dcol = cols[None, :] - cols[:, None] + NA_COLS - 1
    oh_c = (dcol[None] == np.arange(2 * NA_COLS - 1)[:, None, None]).astype(np.float32)
    by_col = jnp.einsum('hrd,dqk->hrqk', rpb.astype(F32), oh_c, precision=HIGHEST)
    tabs = []
    for t in (0, min(1, nq - 1), nq - 1):
        ks = int(np.clip(t * NA_QROWS - NA_ROWS // 2, 0, rows - NA_KROWS))
        qr = t * NA_QROWS + qr_rel
        kr = ks + kr_rel
        r0 = np.clip(qr - NA_ROWS // 2, 0, rows - NA_ROWS)
        ok = ((kr[None, :] >= r0[:, None]) & (kr[None, :] < r0[:, None] + NA_ROWS)
              & (kc[None, :] >= win[:, None]) & (kc[None, :] < win[:, None] + NA_COLS))
        drow = (ks + np.arange(NA_KROWS))[None, :] - (t * NA_QROWS + np.arange(NA_QROWS))[:, None] + NA_ROWS - 1
        oh_r = (drow[None] == np.arange(2 * NA_ROWS - 1)[:, None, None]).astype(np.float32)
        bias = jnp.einsum('rab,hrqk->haqbk', oh_r, by_col, precision=HIGHEST).reshape(-1, Q_TILE, NA_KEYS)
        tabs.append(jnp.where(jnp.asarray(ok)[None], bias, NEG_INF))
    tabs.append(jnp.full_like(tabs[0], NEG_INF))
    return jnp.stack(tabs)


def _layer_weights(w_in, b_w_q_up, b_w_kv_up, w_branch):
    cuts = np.cumsum([512, 128, 128, 512, 256, 32, 512, 512, 512, 512, 512, 512, 512, 512])
    (a_q, a_k, a_v, b_cq, b_ckv, b_kr, c_q, c_i, c_ff, c_fb, c_g, d_q, d_k, d_v, gates) = jnp.split(w_in, cuts, axis=1)
    d = w_in.shape[0]
    aq = a_q.reshape(d, A_HEADS, HEAD_DIM)
    aq = jnp.stack([aq[:, 0:4], aq[:, 4:8]], axis=2).reshape(d, A_HEADS * HEAD_DIM)
    w = jnp.concatenate([gates, aq, a_k, a_v, b_ckv, b_cq, c_q, c_i, c_ff, c_fb, c_g, d_q, d_k, d_v], axis=1)
    wkr = jnp.zeros((d, LANES), F32).at[:, B_NOPE:B_NOPE + B_ROPE].set(b_kr)
    wq = b_w_q_up.reshape(B_Q_RANK, B_HEADS, B_NOPE + B_ROPE)
    wq = jnp.pad(wq, ((0, 0), (0, 0), (0, LANES - B_NOPE - B_ROPE))).reshape(B_Q_RANK, B_HEADS * LANES)
    wkv = b_w_kv_up.reshape(B_KV_RANK, B_HEADS, B_NOPE + B_V)
    wk = jnp.pad(wkv[:, :, :B_NOPE], ((0, 0), (0, 0), (0, LANES - B_NOPE))).reshape(B_KV_RANK, B_HEADS * LANES)
    wv = wkv[:, :, B_NOPE:].reshape(B_KV_RANK, B_HEADS * B_V)
    wb0 = w_branch[0].reshape(A_HEADS, HEAD_DIM, -1)
    wb0 = jnp.stack([wb0[0:4], wb0[4:8]], axis=1).reshape(A_HEADS * HEAD_DIM, -1)
    wb = jnp.concatenate([wb0[None], w_branch[1:]], axis=0)
    cast = lambda t: t.astype(BF16)
    return cast(w), cast(wkr), cast(wq), cast(wk), cast(wv), cast(wb)


def _group_matrix(bounds):
    gid = np.zeros((LANES,), np.int32)
    size = np.zeros((LANES,), np.float32)
    for g, (lo, hi) in enumerate(bounds):
        gid[lo:hi] = g
        size[lo:hi] = hi - lo
    gmat = (gid[:, None] == gid[None, :]).astype(np.float32)
    return jnp.asarray(gmat, dtype=BF16), jnp.asarray(1.0 / size, dtype=F32).reshape(1, LANES)


def _pad_lanes(v, off=0):
    return jnp.zeros((1, LANES), F32).at[0, off:off + v.shape[0]].set(v.astype(F32))


def kernel(x, c, ctx, c_ctx, w_mod, b_mod, g_norm_mix, g_norm_ffn, w_in, a_gq, a_gk, a_sink, b_g_qa, b_g_kva,
           b_w_q_up, b_w_kv_up, b_gq, b_gk, c_lb_logits, c_g_out, d_gq, d_gk, d_rpb, w_branch, w_out,
           router_w, router_b, w_exp_gate, w_exp_up, w_exp_down):
    nb, l, d = x.shape
    lc = ctx.shape[1]
    s = lc + l
    ntok = nb * s
    depth = w_in.shape[0]
    assert l % Q_TILE == 0 and lc % Q_TILE == 0 and (l // GRID_W) >= NA_KROWS and s % 64 == 0

    xs = jnp.concatenate([ctx, x], axis=1).reshape(ntok, d)

    mod_rows = 16
    cvecs = jnp.zeros((mod_rows, d), F32).at[:nb].set(c).at[nb].set(c_ctx)
    mods = _adaln(cvecs, w_mod, b_mod).reshape(depth, mod_rows, 6, d)

    cos_a, sin_a = _rope_tables(l, lc, HEAD_DIM, 0, HEAD_DIM)
    cos_b, sin_b = _rope_tables(l, lc, B_ROPE, B_NOPE, LANES)
    gmat64, _ = _group_matrix([(0, 64), (64, 128)])
    gmat_b, inv_b = _group_matrix([(0, B_NOPE), (B_NOPE, B_NOPE + B_ROPE), (B_NOPE + B_ROPE, LANES)])
    lb_p = jax.nn.softmax(c_lb_logits.astype(F32), axis=1)
    lb_all = jnp.clip(jnp.cumsum(lb_p, axis=1) - lb_p[:, :1], 0.0, 1.0)
    rw_t = router_w.astype(F32).T
    rb = router_b.astype(F32).reshape(N_EXPERTS, 1)

    for lyr in range(depth):
        w, wkr, wq, wk, wv, wb = _layer_weights(w_in[lyr], b_w_q_up[lyr], b_w_kv_up[lyr], w_branch[lyr])
        m = mods[lyr]
        proj, kr = _in_proj(xs, m, g_norm_mix[lyr], w, wkr, nb, s, lc)

        two = lambda g: jnp.concatenate([g, g]).astype(F32)
        gvecs = jnp.stack([two(a_gq[lyr]), two(a_gk[lyr]), two(d_gq[lyr]), two(d_gk[lyr])])
        qk_ad = _prep_ad(proj, cos_a, sin_a, gvecs, gmat64, s)
        qb, kb, vb = _prep_b(proj, kr, wq, wk, wv, b_g_qa[lyr].reshape(1, -1), b_g_kva[lyr].reshape(1, -1),
                             _pad_lanes(b_gq[lyr]), _pad_lanes(b_gk[lyr]), gmat_b, inv_b, cos_b, sin_b, s)

        oa = _attn_a(a_sink[lyr].astype(F32), qk_ad, proj, nb, s, lc)
        ob = _attn_b(qb, kb, vb, nb, s, lc)
        ofw, obw = _gla(proj, lb_all[:, lyr], nb, s, lc)
        od = _attn_d(qk_ad, proj, _na_bias(d_rpb[lyr], l // GRID_W), nb, s, lc)

        y = _branch_mix(oa, ob, ofw, obw, od, proj, wb, c_g_out[lyr].reshape(1, -1).astype(F32), s)
        xn, h2c, h2p = _out_proj(y, xs, m, w_out[lyr].astype(BF16), g_norm_ffn[lyr].reshape(1, -1), nb, s, lc)

        idx, wts = _router(h2c, rw_t, rb)
        texp, entries, row_w = _route_plan(idx, wts, ntok)
        ybuf = _experts(h2p, texp, entries, row_w, w_exp_gate[lyr].astype(BF16),
                        w_exp_up[lyr].astype(BF16), w_exp_down[lyr].astype(BF16), ntok)
        xs = _combine(xn, ybuf, m, nb, s, lc)

    return xs.reshape(nb, s, d)[:, lc:]
```

```python
import functools

import numpy as np
import jax
import jax.numpy as jnp
from jax import lax
from jax.experimental import pallas as pl
from jax.experimental.pallas import tpu as pltpu

F32 = jnp.float32
BF16 = jnp.bfloat16
HIGHEST = lax.Precision.HIGHEST

GRID_W = 64
HEAD_DIM = 64
ROPE_BASE = 10000.0
EPS = 1e-6
F_MIN = 1e-6
NEG_INF = -1e30
A_HEADS, A_KV_HEADS, A_WINDOW = 8, 2, 128
B_HEADS, B_NOPE, B_ROPE, B_V, B_Q_RANK, B_KV_RANK = 8, 64, 32, 64, 512, 256
C_HEADS, C_DK, C_DV = 4, 128, 128
D_HEADS, NA_ROWS, NA_COLS = 8, 8, 16
N_BRANCH, BRANCH_W = 4, 512
N_EXPERTS, N_GROUPS, TOP_K, D_EXPERT = 16, 4, 2, 1024

LANES = 128
VMEM_LIMIT = 56 * 1024 * 1024

COL_GATES = 0
COL_AQ, COL_AK, COL_AV, COL_BCKV, COL_BCQ = 8192, 8704, 8832, 8960, 9216
COL_CQ, COL_CI, COL_CFF, COL_CFB, COL_CG = 9728, 10240, 10752, 11264, 11776
COL_DQ, COL_DK, COL_DV = 12288, 12800, 13312
PROJ_W = COL_DV + 512

Q_TILE = 256
B_STEP_HEADS = 4
GLA_CHUNK = 128
CHUNK_ROWS = 16
PACK_ROWS = 8
EXPERT_TILE = 512


def _cparams(sem, vmem=VMEM_LIMIT):
    return pltpu.CompilerParams(dimension_semantics=sem, vmem_limit_bytes=vmem)


def _sigmoid(x):
    return 1.0 / (1.0 + jnp.exp(-x))


def _silu(x):
    return x * _sigmoid(x)


def _gate_sigmoid(x):
    return 0.5 * jnp.tanh(0.5 * x) + 0.5


def _gate_silu(x):
    return x * _gate_sigmoid(x)


def _store_packed(ref, x, tile, row0=0):
    for j in range(PACK_ROWS):
        lo = x[:, j * LANES:(j + 1) * LANES].astype(BF16).astype(F32)
        hi = x[:, (j + PACK_ROWS) * LANES:(j + PACK_ROWS + 1) * LANES].astype(BF16).astype(F32)
        word = ((pltpu.bitcast(hi, jnp.uint32) & jnp.uint32(0xFFFF0000))
                | lax.shift_right_logical(pltpu.bitcast(lo, jnp.uint32), jnp.uint32(16)))
        ref[pl.ds(row0 * PACK_ROWS + j, tile, stride=PACK_ROWS), :] = word


def _load_packed(ref, j, tile):
    word = ref[pl.ds(j, tile, stride=PACK_ROWS), :]
    lo = pltpu.bitcast(lax.shift_left(word, jnp.uint32(16)), F32)
    hi = pltpu.bitcast(word & jnp.uint32(0xFFFF0000), F32)
    return lo, hi


def _rot_pairs(x, half):
    n = x.shape[-1]
    ax = x.ndim - 1
    lane = lax.broadcasted_iota(jnp.int32, x.shape, ax)
    fwd = pltpu.roll(x, n - half, axis=ax)
    bwd = pltpu.roll(x, half, axis=ax)
    return jnp.where((lane & (2 * half - 1)) < half, fwd, bwd)


def _row_mod(i, mb_ref, mc_ref, idx, tile, tiles_per_batch, lc, row0=0, nrows=None):
    nrows = tile if nrows is None else nrows
    row = (i % tiles_per_batch) * tile + row0 + lax.broadcasted_iota(jnp.int32, (nrows, 1), 0)
    return jnp.where(row < lc, mc_ref[0, idx:idx + 1, :], mb_ref[0, idx:idx + 1, :])


def _mod_norm(x, g, shift, scale):
    ms = jnp.mean(x * x, axis=-1, keepdims=True)
    return x * lax.rsqrt(ms + EPS) * g * (1.0 + scale) + shift


def _adaln_kernel(c_ref, w_ref, b_ref, o_ref):
    cv = c_ref[...]
    o_ref[0] = jnp.dot(_silu(cv), w_ref[0], preferred_element_type=F32, precision=HIGHEST) + b_ref[0]


def _adaln(cvecs, w_mod, b_mod):
    depth, d, n6 = w_mod.shape
    rows = cvecs.shape[0]
    tn = 1024
    return pl.pallas_call(
        _adaln_kernel,
        grid=(depth, n6 // tn),
        in_specs=[pl.BlockSpec((rows, d), lambda l, j: (0, 0)),
                  pl.BlockSpec((1, d, tn), lambda l, j: (l, 0, j)),
                  pl.BlockSpec((1, 1, tn), lambda l, j: (l, 0, j))],
        out_specs=pl.BlockSpec((1, rows, tn), lambda l, j: (l, 0, j)),
        out_shape=jax.ShapeDtypeStruct((depth, rows, n6), F32),
        compiler_params=_cparams(("parallel", "parallel")),
        name="adaln",
    )(cvecs, w_mod, b_mod.reshape(depth, 1, n6))


def _inproj_kernel(x_ref, mb_ref, mc_ref, g_ref, w_ref, wkr_ref, o_ref, kr_ref, h_scr, *, tile, tpb, lc):
    i = pl.program_id(0)

    @pl.when(pl.program_id(1) == 0)
    def _():
        shift = _row_mod(i, mb_ref, mc_ref, 0, tile, tpb, lc)
        scale = _row_mod(i, mb_ref, mc_ref, 1, tile, tpb, lc)
        hb = _mod_norm(x_ref[...], g_ref[...], shift, scale).astype(BF16)
        h_scr[...] = hb
        kr_ref[...] = jnp.dot(hb, wkr_ref[...], preferred_element_type=F32).astype(BF16)

    o_ref[...] = jnp.dot(h_scr[...], w_ref[...], preferred_element_type=F32).astype(BF16)


def _in_proj(xs, mods, g, w, wkr, nb, s, lc):
    ntok, d = xs.shape
    tile = s // 4
    tpb = s // tile
    tn = 1536
    kern = functools.partial(_inproj_kernel, tile=tile, tpb=tpb, lc=lc)
    return pl.pallas_call(
        kern,
        grid=(ntok // tile, PROJ_W // tn),
        in_specs=[pl.BlockSpec((tile, d), lambda i, j: (i, 0)),
                  pl.BlockSpec((1, 6, d), lambda i, j: (i // tpb, 0, 0)),
                  pl.BlockSpec((1, 6, d), lambda i, j: (nb, 0, 0)),
                  pl.BlockSpec((1, d), lambda i, j: (0, 0)),
                  pl.BlockSpec((d, tn), lambda i, j: (0, j)),
                  pl.BlockSpec((d, LANES), lambda i, j: (0, 0))],
        out_specs=[pl.BlockSpec((tile, tn), lambda i, j: (i, j)),
                   pl.BlockSpec((tile, LANES), lambda i, j: (i, 0))],
        out_shape=[jax.ShapeDtypeStruct((ntok, PROJ_W), BF16),
                   jax.ShapeDtypeStruct((ntok, LANES), BF16)],
        scratch_shapes=[pltpu.VMEM((tile, d), BF16)],
        compiler_params=_cparams(("parallel", "arbitrary")),
        name="in_proj",
    )(xs, mods, mods, g.reshape(1, d), w, wkr)


def _group_norm(x, gmat, inv_size, gvec):
    ss = jnp.dot((x * x).astype(BF16), gmat, preferred_element_type=F32)
    return x * lax.rsqrt(ss * inv_size + EPS) * gvec


def _prep_ad_kernel(aq_ref, ak_ref, d_ref, cos_ref, sin_ref, g_ref, gmat_ref, o_ref):
    gmat = gmat_ref[...]
    cos = cos_ref[...]
    sin = sin_ref[...]
    scale = HEAD_DIM ** -0.5
    inv = 1.0 / HEAD_DIM
    for blk in range(5):
        x = (aq_ref[:, blk * LANES:(blk + 1) * LANES] if blk < 4 else ak_ref[...]).astype(F32)
        gi = 0 if blk < 4 else 1
        xn = _group_norm(x, gmat, inv, g_ref[gi:gi + 1, :])
        xr = xn * cos + _rot_pairs(xn, HEAD_DIM // 2) * sin
        if blk < 4:
            xr = xr * scale
        ob = blk if blk < 4 else 12
        o_ref[:, ob * LANES:(ob + 1) * LANES] = xr.astype(BF16)
    for blk in range(8):
        x = d_ref[:, blk * LANES:(blk + 1) * LANES].astype(F32)
        gi = 2 if blk < 4 else 3
        xn = _group_norm(x, gmat, inv, g_ref[gi:gi + 1, :])
        if blk < 4:
            xn = xn * scale
        o_ref[:, (4 + blk) * LANES:(5 + blk) * LANES] = xn.astype(BF16)


def _prep_ad(proj, cos, sin, gvecs, gmat, s):
    ntok = proj.shape[0]
    tile = s // 4
    tpb = s // tile
    wa, wd = 4 * LANES, 8 * LANES
    return pl.pallas_call(
        _prep_ad_kernel,
        grid=(ntok // tile,),
        in_specs=[pl.BlockSpec((tile, wa), lambda i: (i, COL_AQ // wa)),
                  pl.BlockSpec((tile, LANES), lambda i: (i, COL_AK // LANES)),
                  pl.BlockSpec((tile, wd), lambda i: (i, COL_DQ // wd)),
                  pl.BlockSpec((tile, LANES), lambda i: (i % tpb, 0)),
                  pl.BlockSpec((tile, LANES), lambda i: (i % tpb, 0)),
                  pl.BlockSpec((4, LANES), lambda i: (0, 0)),
                  pl.BlockSpec((LANES, LANES), lambda i: (0, 0))],
        out_specs=pl.BlockSpec((tile, 13 * LANES), lambda i: (i, 0)),
        out_shape=jax.ShapeDtypeStruct((ntok, 13 * LANES), BF16),
        compiler_params=_cparams(("parallel",)),
        name="prep_ad",
    )(proj, proj, proj, cos, sin, gvecs, gmat)


def _prep_b_kernel(cq_ref, ckv_ref, kr_ref, wq_ref, wk_ref, wv_ref, gqa_ref, gkva_ref, gq_ref, gk_ref,
                   gmat_ref, inv_ref, cos_ref, sin_ref, q_ref, k_ref, v_ref):
    gmat = gmat_ref[...]
    inv = inv_ref[...]
    cos = cos_ref[...]
    sin = sin_ref[...]
    scale = (B_NOPE + B_ROPE) ** -0.5

    def rms(x, g):
        return x * lax.rsqrt(jnp.mean(x * x, axis=-1, keepdims=True) + EPS) * g

    cq = rms(cq_ref[...].astype(F32), gqa_ref[...]).astype(BF16)
    ckv = rms(ckv_ref[...].astype(F32), gkva_ref[...]).astype(BF16)
    q = jnp.dot(cq, wq_ref[...], preferred_element_type=F32)
    kn = jnp.dot(ckv, wk_ref[...], preferred_element_type=F32)
    v_ref[...] = jnp.dot(ckv, wv_ref[...], preferred_element_type=F32).astype(BF16)
    kro = _group_norm(kr_ref[...].astype(F32), gmat, inv, gk_ref[...])
    kro = kro * cos + _rot_pairs(kro, B_ROPE // 2) * sin
    for h in range(B_HEADS):
        sl = slice(h * LANES, (h + 1) * LANES)
        qh = _group_norm(q[:, sl], gmat, inv, gq_ref[...])
        qh = (qh * cos + _rot_pairs(qh, B_ROPE // 2) * sin) * scale
        q_ref[:, sl] = qh.astype(BF16)
        kh = _group_norm(kn[:, sl], gmat, inv, gk_ref[...]) + kro
        k_ref[:, sl] = kh.astype(BF16)


def _prep_b(proj, kr, wq, wk, wv, gqa, gkva, gq, gk, gmat, inv, cos, sin, s):
    ntok = proj.shape[0]
    tile = s // 4
    tpb = s // tile
    hw = B_HEADS * LANES
    full = lambda shape: pl.BlockSpec(shape, lambda i: (0, 0))
    return pl.pallas_call(
        _prep_b_kernel,
        grid=(ntok // tile,),
        in_specs=[pl.BlockSpec((tile, B_Q_RANK), lambda i: (i, COL_BCQ // B_Q_RANK)),
                  pl.BlockSpec((tile, B_KV_RANK), lambda i: (i, COL_BCKV // B_KV_RANK)),
                  pl.BlockSpec((tile, LANES), lambda i: (i, 0)),
                  full((B_Q_RANK, hw)), full((B_KV_RANK, hw)), full((B_KV_RANK, B_HEADS * B_V)),
                  full((1, B_Q_RANK)), full((1, B_KV_RANK)), full((1, LANES)), full((1, LANES)),
                  full((LANES, LANES)), full((1, LANES)),
                  pl.BlockSpec((tile, LANES), lambda i: (i % tpb, 0)),
                  pl.BlockSpec((tile, LANES), lambda i: (i % tpb, 0))],
        out_specs=[pl.BlockSpec((tile, hw), lambda i: (i, 0)),
                   pl.BlockSpec((tile, hw), lambda i: (i, 0)),
                   pl.BlockSpec((tile, B_HEADS * B_V), lambda i: (i, 0))],
        out_shape=[jax.ShapeDtypeStruct((ntok, hw), BF16),
                   jax.ShapeDtypeStruct((ntok, hw), BF16),
                   jax.ShapeDtypeStruct((ntok, B_HEADS * B_V), BF16)],
        compiler_params=_cparams(("parallel",)),
        name="prep_b",
    )(proj, proj, kr, wq, wk, wv, gqa, gkva, gq, gk, gmat, inv, cos, sin)


def _nt_dot(a, b):
    return lax.dot_general(a, b, (((1,), (1,)), ((), ())), preferred_element_type=F32)


def _half_mask(shape, half):
    lane = lax.broadcasted_iota(jnp.int32, shape, len(shape) - 1)
    return (lane >= HEAD_DIM) if half else (lane < HEAD_DIM)


def _joint_attend(s, sc, vw, vc, sink=None):
    m = jnp.maximum(jnp.max(s, axis=-1, keepdims=True), jnp.max(sc, axis=-1, keepdims=True))
    if sink is not None:
        m = jnp.maximum(m, sink)
    p = jnp.exp(s - m)
    pc = jnp.exp(sc - m)
    den = jnp.sum(p, axis=-1, keepdims=True) + jnp.sum(pc, axis=-1, keepdims=True)
    if sink is not None:
        den = den + jnp.exp(sink - m)
    o = (jnp.dot(p.astype(BF16), vw, preferred_element_type=F32)
         + jnp.dot(pc.astype(BF16), vc, preferred_element_type=F32))
    return o / den


def _attn_a_kernel(sink_ref, q_ref, k_ref, v_ref, o_ref, *, lc, l):
    qt = pl.program_id(1)
    span = Q_TILE + 2 * A_WINDOW
    qstart = (qt - 1) * Q_TILE
    ws = jnp.clip(qstart - A_WINDOW, 0, l - span)
    wrow = pl.multiple_of(lc + ws, LANES)
    kw = k_ref[pl.ds(wrow, span), :]
    vw = v_ref[pl.ds(wrow, span), :]
    kc = k_ref[0:lc, :]
    vc = v_ref[0:lc, :]
    kpos = ws + lax.broadcasted_iota(jnp.int32, (Q_TILE, span), 1)
    qoff = jnp.where(qt > 0, qstart, -(1 << 20))
    qpos = qoff + lax.broadcasted_iota(jnp.int32, (Q_TILE, span), 0)
    ok = jnp.abs(kpos - qpos) <= A_WINDOW
    for blk in range(4):
        qb = q_ref[:, blk * LANES:(blk + 1) * LANES]
        halves = []
        for half in range(2):
            head = blk + 4 * half
            qm = jnp.where(_half_mask(qb.shape, half), qb, jnp.zeros_like(qb))
            s = jnp.where(ok, _nt_dot(qm, kw), NEG_INF)
            sc = _nt_dot(qm, kc)
            halves.append(_joint_attend(s, sc, vw, vc, sink_ref[head]))
        out = jnp.where(_half_mask(halves[0].shape, 0), halves[0], halves[1])
        o_ref[:, blk * LANES:(blk + 1) * LANES] = out.astype(BF16)


def _attn_a(sink, qk, proj, nb, s, lc):
    ntok = qk.shape[0]
    nq = s // Q_TILE
    kern = functools.partial(_attn_a_kernel, lc=lc, l=s - lc)
    return pl.pallas_call(
        kern,
        grid_spec=pltpu.PrefetchScalarGridSpec(
            num_scalar_prefetch=0,
            grid=(nb, nq),
            in_specs=[pl.BlockSpec(memory_space=pltpu.SMEM),
                      pl.BlockSpec((Q_TILE, 4 * LANES), lambda b, t: (b * nq + t, 0)),
                      pl.BlockSpec((s, LANES), lambda b, t: (b, 12)),
                      pl.BlockSpec((s, LANES), lambda b, t: (b, COL_AV // LANES))],
            out_specs=pl.BlockSpec((Q_TILE, 4 * LANES), lambda b, t: (b * nq + t, 0)),
        ),
        out_shape=jax.ShapeDtypeStruct((ntok, 4 * LANES), BF16),
        compiler_params=_cparams(("parallel", "arbitrary")),
        name="attn_a",
    )(sink, qk, qk, proj)


def _attn_b_kernel(q_ref, k_ref, v_ref, o_ref, s_scr, p_scr, *, s, lc):
    qt = pl.program_id(2)

    def head(hh, nkeys):
        hl = slice(hh * LANES, (hh + 1) * LANES)
        vl = slice((hh // 2) * LANES, (hh // 2 + 1) * LANES)
        q = q_ref[:, hl]
        for c in range(nkeys // Q_TILE):
            rows = slice(c * Q_TILE, (c + 1) * Q_TILE)
            s_scr[hh, :, rows] = _nt_dot(q, k_ref[rows, hl])
        mx = s_scr[hh, :, 0:LANES]
        for j in range(1, nkeys // LANES):
            mx = jnp.maximum(mx, s_scr[hh, :, j * LANES:(j + 1) * LANES])
        m = jnp.broadcast_to(jnp.max(mx, axis=-1, keepdims=True), (Q_TILE, LANES))
        lsum = jnp.zeros((Q_TILE, LANES), F32)
        for j in range(nkeys // LANES):
            cols = slice(j * LANES, (j + 1) * LANES)
            p = jnp.exp(s_scr[hh, :, cols] - m)
            lsum = lsum + p
            p_scr[hh, :, cols] = p.astype(BF16)
        den = jnp.sum(lsum, axis=-1, keepdims=True)
        o = jnp.dot(p_scr[hh, :, 0:nkeys], v_ref[0:nkeys, vl], preferred_element_type=F32)
        return o / den

    def both(nkeys):
        for pair in range(B_STEP_HEADS // 2):
            r0 = head(2 * pair, nkeys)
            r1 = head(2 * pair + 1, nkeys)
            o_ref[:, pair * LANES:(pair + 1) * LANES] = jnp.where(_half_mask(r0.shape, 0), r0, r1).astype(BF16)

    @pl.when(qt == 0)
    def _():
        both(lc)

    @pl.when(qt > 0)
    def _():
        both(s)


def _attn_b(qb, kb, vb, nb, s, lc):
    ntok = qb.shape[0]
    nq = s // Q_TILE
    kern = functools.partial(_attn_b_kernel, s=s, lc=lc)
    hs = B_STEP_HEADS
    return pl.pallas_call(
        kern,
        grid=(nb, B_HEADS // hs, nq),
        in_specs=[pl.BlockSpec((Q_TILE, hs * LANES), lambda b, p, t: (b * nq + t, p)),
                  pl.BlockSpec((s, hs * LANES), lambda b, p, t: (b, p)),
                  pl.BlockSpec((s, hs * B_V), lambda b, p, t: (b, p))],
        out_specs=pl.BlockSpec((Q_TILE, hs * B_V), lambda b, p, t: (b * nq + t, p)),
        out_shape=jax.ShapeDtypeStruct((ntok, B_HEADS * B_V), BF16),
        scratch_shapes=[pltpu.VMEM((hs, Q_TILE, s), F32), pltpu.VMEM((hs, Q_TILE, s), BF16)],
        compiler_params=_cparams(("parallel", "parallel", "arbitrary")),
        name="attn_b",
    )(qb, kb, vb)


NA_QROWS = Q_TILE // GRID_W
NA_KROWS = NA_QROWS + NA_ROWS
NA_KEYS = NA_KROWS * GRID_W


def _attn_d_kernel(q_ref, k_ref, v_ref, bias_ref, o_ref, *, lc, rows):
    qt = pl.program_id(1)
    ks = jnp.clip((qt - 1) * NA_QROWS - NA_ROWS // 2, 0, rows - NA_KROWS)
    wrow = pl.multiple_of(lc + ks * GRID_W, Q_TILE)
    for blk in range(4):
        sl = slice(blk * LANES, (blk + 1) * LANES)
        qb = q_ref[:, sl]
        kw = k_ref[pl.ds(wrow, NA_KEYS), sl]
        vw = v_ref[pl.ds(wrow, NA_KEYS), sl]
        kc = k_ref[0:lc, sl]
        vc = v_ref[0:lc, sl]
        halves = []
        for half in range(2):
            head = 2 * blk + half
            qm = jnp.where(_half_mask(qb.shape, half), qb, jnp.zeros_like(qb))
            s = _nt_dot(qm, kw) + bias_ref[0, head]
            sc = _nt_dot(qm, kc)
            halves.append(_joint_attend(s, sc, vw, vc))
        out = jnp.where(_half_mask(halves[0].shape, 0), halves[0], halves[1])
        o_ref[:, sl] = out.astype(BF16)


def _na_variant(t, nq):
    return jnp.where(t == 0, 3, jnp.where(t == 1, 0, jnp.where(t == nq - 1, 2, 1)))


def _attn_d(qk, proj, bias, nb, s, lc):
    ntok = qk.shape[0]
    nq = s // Q_TILE
    rows = (s - lc) // GRID_W
    kern = functools.partial(_attn_d_kernel, lc=lc, rows=rows)
    w = 4 * LANES
    return pl.pallas_call(
        kern,
        grid=(nb, nq),
        in_specs=[pl.BlockSpec((Q_TILE, w), lambda b, t: (b * nq + t, 1)),
                  pl.BlockSpec((s, w), lambda b, t: (b, 2)),
                  pl.BlockSpec((s, w), lambda b, t: (b, COL_DV // w)),
                  pl.BlockSpec((1, D_HEADS, Q_TILE, NA_KEYS), lambda b, t: (_na_variant(t, nq), 0, 0, 0))],
        out_specs=pl.BlockSpec((Q_TILE, w), lambda b, t: (b * nq + t, 0)),
        out_shape=jax.ShapeDtypeStruct((ntok, w), BF16),
        compiler_params=_cparams(("parallel", "arbitrary")),
        name="attn_d",
    )(qk, qk, proj, bias)


def _cumsum_rows(x, reverse):
    n = x.shape[0]
    row = lax.broadcasted_iota(jnp.int32, x.shape, 0)
    step = 1
    while step < n:
        if reverse:
            x = x + jnp.where(row < n - step, pltpu.roll(x, n - step, axis=0), 0.0)
        else:
            x = x + jnp.where(row >= step, pltpu.roll(x, step, axis=0), 0.0)
        step *= 2
    return x


def _block_ref(b, size, idx):
    n, w = b.shape
    b3 = b.reshape(n // size, size, w)
    return jnp.broadcast_to(b3[:, idx:idx + 1, :], b3.shape).reshape(n, w)


def _gla_chunk(q, v, fpre, lb, st_ref, reverse):
    c = q.shape[0]
    fg = lb + (1.0 - lb) * _sigmoid(fpre)
    logf = jnp.log(jnp.maximum(fg, F_MIN))
    k = 1.0 - fg
    qs = _silu(q)
    b = _cumsum_rows(logf, reverse)
    row = lax.broadcasted_iota(jnp.int32, (c, c), 0)
    col = lax.broadcasted_iota(jnp.int32, (c, c), 1)
    xor = row ^ col
    after = (row < col) if reverse else (row > col)

    base = 8
    bref = _block_ref(b, base, base // 2 if reverse else base // 2 - 1)
    qe = (qs * jnp.exp(b - bref)).astype(BF16)
    ke = (k * jnp.exp(bref - b)).astype(BF16)
    a = jnp.where((xor < base) & (after | (row == col)), _nt_dot(qe, ke), 0.0)
    m = base
    while m < c:
        bref = _block_ref(b, 2 * m, m if reverse else m - 1)
        qe = (qs * jnp.exp(jnp.minimum(b - bref, 0.0))).astype(BF16)
        ke = (k * jnp.exp(jnp.minimum(bref - b, 0.0))).astype(BF16)
        a = jnp.where((xor >= m) & (xor < 2 * m) & after, _nt_dot(qe, ke), a)
        m *= 2

    st = st_ref[...]
    vb = v.astype(BF16)
    o = (jnp.dot(a.astype(BF16), vb, preferred_element_type=F32)
         + _nt_dot((qs * jnp.exp(b)).astype(BF16), st.astype(BF16)))
    btot = b[0:1, :] if reverse else b[c - 1:c, :]
    kdec = (k * jnp.exp(btot - b)).astype(BF16)
    upd = lax.dot_general(vb, kdec, (((0,), (0,)), ((), ())), preferred_element_type=F32)
    st_ref[...] = st * jnp.exp(btot) + upd
    return o


def _gla_kernel(qf_ref, vf_ref, ff_ref, qb_ref, vb_ref, fb_ref, lb_ref, of_ref, ob_ref, stf, stb):
    @pl.when(pl.program_id(1) == 0)
    def _():
        stf[...] = jnp.zeros_like(stf)
        stb[...] = jnp.zeros_like(stb)

    for h in range(C_HEADS):
        sl = slice(h * LANES, (h + 1) * LANES)
        of_ref[:, sl] = _gla_chunk(qf_ref[:, sl].astype(F32), vf_ref[:, sl].astype(F32),
                                   ff_ref[:, sl].astype(F32), lb_ref[0:1, sl], stf.at[h], False)
        ob_ref[:, sl] = _gla_chunk(qb_ref[:, sl].astype(F32), vb_ref[:, sl].astype(F32),
                                   fb_ref[:, sl].astype(F32), lb_ref[1:2, sl], stb.at[h], True)


def _gla(proj, lb, nb, s, lc):
    ntok = proj.shape[0]
    c = GLA_CHUNK
    ns = s // c
    nctx = lc // c
    w = C_HEADS * C_DK

    def bw_chunk(t):
        return jnp.where(t < nctx, nctx - 1 - t, ns - 1 - (t - nctx))

    def fw(col):
        return pl.BlockSpec((c, w), lambda b, t: (b * ns + t, col // w))

    def bw(col):
        return pl.BlockSpec((c, w), lambda b, t: (b * ns + bw_chunk(t), col // w))

    return pl.pallas_call(
        _gla_kernel,
        grid=(nb, ns),
        in_specs=[fw(COL_CQ), fw(COL_CI), fw(COL_CFF), bw(COL_CQ), bw(COL_CI), bw(COL_CFB),
                  pl.BlockSpec((2, w), lambda b, t: (0, 0))],
        out_specs=[pl.BlockSpec((c, w), lambda b, t: (b * ns + t, 0)),
                   pl.BlockSpec((c, w), lambda b, t: (b * ns + bw_chunk(t), 0))],
        out_shape=[jax.ShapeDtypeStruct((ntok, C_HEADS * C_DV), F32)] * 2,
        scratch_shapes=[pltpu.VMEM((C_HEADS, C_DV, C_DK), F32), pltpu.VMEM((C_HEADS, C_DV, C_DK), F32)],
        compiler_params=_cparams(("parallel", "arbitrary")),
        name="gla",
    )(proj, proj, proj, proj, proj, proj, lb)


def _mix_kernel(oa_ref, ob_ref, ofw_ref, obw_ref, cg_ref, od_ref, g0_ref, g1_ref, g2_ref, g3_ref,
                wb_ref, gout_ref, y_ref, oc_scr):
    @pl.when(pl.program_id(1) == 0)
    def _():
        gout = gout_ref[...]
        for h in range(C_HEADS):
            sl = slice(h * LANES, (h + 1) * LANES)
            o = ofw_ref[:, sl] + obw_ref[:, sl]
            on = o * lax.rsqrt(jnp.mean(o * o, axis=-1, keepdims=True) + EPS) * gout
            oc_scr[:, sl] = (on * _gate_silu(cg_ref[:, sl].astype(F32))).astype(BF16)

    outs = (oa_ref[...], ob_ref[...], oc_scr[...], od_ref[...])
    gates = (g0_ref, g1_ref, g2_ref, g3_ref)
    y = None
    for br in range(N_BRANCH):
        t = _gate_sigmoid(gates[br][...].astype(F32)) * jnp.dot(outs[br], wb_ref[br], preferred_element_type=F32)
        y = t if y is None else y + t
    y_ref[...] = y.astype(BF16)


def _branch_mix(oa, ob, ofw, obw, od, proj, wb, gout, s):
    ntok = oa.shape[0]
    d = wb.shape[2]
    tile = s // 8
    tn = 1024
    row512 = pl.BlockSpec((tile, 512), lambda i, n: (i, 0))

    def gate(br):
        return pl.BlockSpec((tile, tn), lambda i, n: (i, (COL_GATES + br * d) // tn + n))

    return pl.pallas_call(
        _mix_kernel,
        grid=(ntok // tile, d // tn),
        in_specs=[row512, row512, row512, row512,
                  pl.BlockSpec((tile, 512), lambda i, n: (i, COL_CG // 512)),
                  row512, gate(0), gate(1), gate(2), gate(3),
                  pl.BlockSpec((N_BRANCH, BRANCH_W, tn), lambda i, n: (0, 0, n)),
                  pl.BlockSpec((1, LANES), lambda i, n: (0, 0))],
        out_specs=pl.BlockSpec((tile, tn), lambda i, n: (i, n)),
        out_shape=jax.ShapeDtypeStruct((ntok, d), BF16),
        scratch_shapes=[pltpu.VMEM((tile, 512), BF16)],
        compiler_params=_cparams(("parallel", "arbitrary")),
        name="branch_mix",
    )(oa, ob, ofw, obw, proj, od, proj, proj, proj, proj, wb, gout)


def _outproj_kernel(y_ref, x_ref, mb_ref, mc_ref, wo_ref, gffn_ref, xn_ref, h2_ref, h2p_ref, *, tile, tpb, lc):
    i = pl.program_id(0)
    half = tile // 2
    for r0 in (0, half):
        rs = slice(r0, r0 + half)
        mod = lambda idx: _row_mod(i, mb_ref, mc_ref, idx, tile, tpb, lc, r0, half)
        out = jnp.dot(y_ref[rs, :], wo_ref[...], preferred_element_type=F32)
        xn = x_ref[rs, :] + mod(2) * out
        xn_ref[rs, :] = xn
        h2 = _mod_norm(xn, gffn_ref[...], mod(3), mod(4))
        for j in range(CHUNK_ROWS):
            h2_ref[pl.ds(r0 * CHUNK_ROWS + j, half, stride=CHUNK_ROWS), :] = h2[:, j * LANES:(j + 1) * LANES]
        _store_packed(h2p_ref, h2, half, r0)


def _out_proj(y, xs, mods, wo, gffn, nb, s, lc):
    ntok, d = xs.shape
    tile = s // 8
    tpb = s // tile
    kern = functools.partial(_outproj_kernel, tile=tile, tpb=tpb, lc=lc)
    return pl.pallas_call(
        kern,
        grid=(ntok // tile,),
        in_specs=[pl.BlockSpec((tile, d), lambda i: (i, 0)),
                  pl.BlockSpec((tile, d), lambda i: (i, 0)),
                  pl.BlockSpec((1, 6, d), lambda i: (i // tpb, 0, 0)),
                  pl.BlockSpec((1, 6, d), lambda i: (nb, 0, 0)),
                  pl.BlockSpec((d, d), lambda i: (0, 0), pipeline_mode=pl.Buffered(1)),
                  pl.BlockSpec((1, d), lambda i: (0, 0))],
        out_specs=[pl.BlockSpec((tile, d), lambda i: (i, 0)),
                   pl.BlockSpec((tile * CHUNK_ROWS, LANES), lambda i: (i, 0)),
                   pl.BlockSpec((tile * PACK_ROWS, LANES), lambda i: (i, 0))],
        out_shape=[jax.ShapeDtypeStruct((ntok, d), F32),
                   jax.ShapeDtypeStruct((ntok * CHUNK_ROWS, LANES), F32),
                   jax.ShapeDtypeStruct((ntok * PACK_ROWS, LANES), jnp.uint32)],
        compiler_params=_cparams(("parallel",)),
        name="out_proj",
    )(y, xs, mods, mods, wo, gffn)


def _router_kernel(h_ref, rw_ref, rb_ref, idx_ref, w_ref, *, tile):
    logits = None
    for j in range(CHUNK_ROWS):
        hj = h_ref[pl.ds(j, tile, stride=CHUNK_ROWS), :]
        part = lax.dot_general(rw_ref[:, j * LANES:(j + 1) * LANES], hj, (((1,), (1,)), ((), ())),
                               preferred_element_type=F32, precision=HIGHEST)
        logits = part if logits is None else logits + part
    scores = _sigmoid(logits)
    biased = scores + rb_ref[...]
    sc = [scores[e:e + 1, :] for e in range(N_EXPERTS)]
    bi = [biased[e:e + 1, :] for e in range(N_EXPERTS)]
    per = N_EXPERTS // N_GROUPS
    best_g = jnp.zeros((1, tile), jnp.int32)
    best_s = None
    for g in range(N_GROUPS):
        members = bi[g * per:(g + 1) * per]
        gs = None
        for a in range(per):
            for b in range(a + 1, per):
                pair = members[a] + members[b]
                gs = pair if gs is None else jnp.maximum(gs, pair)
        if best_s is None:
            best_s = gs
        else:
            better = gs > best_s
            best_g = jnp.where(better, g, best_g)
            best_s = jnp.where(better, gs, best_s)
    picks = []
    taken = jnp.full((1, tile), -1, jnp.int32)
    for _ in range(TOP_K):
        bv = jnp.full((1, tile), -3e38, F32)
        be = jnp.zeros((1, tile), jnp.int32)
        bw = jnp.zeros((1, tile), F32)
        for e in range(N_EXPERTS):
            val = jnp.where(best_g == e // per, bi[e], NEG_INF)
            better = (val > bv) & (taken != e)
            bv = jnp.where(better, val, bv)
            be = jnp.where(better, e, be)
            bw = jnp.where(better, sc[e], bw)
        picks.append((be, bw))
        taken = be
    tot = picks[0][1] + picks[1][1]
    idx_ref[...] = jnp.concatenate([picks[0][0], picks[1][0]], axis=0)
    w_ref[...] = jnp.concatenate([picks[0][1] / tot, picks[1][1] / tot], axis=0)


def _router(h2c, rw_t, rb):
    ntok = h2c.shape[0] // CHUNK_ROWS
    tile = 1024 if ntok % 1024 == 0 else 256
    d = rw_t.shape[1]
    kern = functools.partial(_router_kernel, tile=tile)
    return pl.pallas_call(
        kern,
        grid=(ntok // tile,),
        in_specs=[pl.BlockSpec((tile * CHUNK_ROWS, LANES), lambda i: (i, 0)),
                  pl.BlockSpec((N_EXPERTS, d), lambda i: (0, 0)),
                  pl.BlockSpec((N_EXPERTS, 1), lambda i: (0, 0))],
        out_specs=[pl.BlockSpec((TOP_K, tile), lambda i: (0, i)),
                   pl.BlockSpec((TOP_K, tile), lambda i: (0, i))],
        out_shape=[jax.ShapeDtypeStruct((TOP_K, ntok), jnp.int32),
                   jax.ShapeDtypeStruct((TOP_K, ntok), F32)],
        compiler_params=_cparams(("parallel",)),
        name="router",
    )(h2c, rw_t, rb)


def _expert_kernel(texp_ref, ent_ref, h_hbm, rw_ref, wg_ref, wu_ref, wd_ref, y_hbm,
                   gbuf0, gbuf1, xbuf, obuf0, obuf1, gsem, ssem, *, ntok, tile):
    i = pl.program_id(0)
    n = pl.num_programs(0)
    slab = tile * PACK_ROWS
    gbufs = (gbuf0, gbuf1)
    obufs = (obuf0, obuf1)

    def rows(r):
        return pl.ds(pl.multiple_of(r * PACK_ROWS, PACK_ROWS), PACK_ROWS)

    def gather_row(t, r, sl):
        e = ent_ref[(t + 1) * tile + r]
        src = jnp.where(e >= 2 * ntok, 0, jnp.where(e >= ntok, e - ntok, e))
        pltpu.make_async_copy(h_hbm.at[rows(src), :], gbufs[sl].at[rows(r), :], gsem.at[sl]).start()

    def scatter_row(t, r, sl):
        e = ent_ref[(t + 1) * tile + r]
        pltpu.make_async_copy(obufs[sl].at[rows(r), :], y_hbm.at[rows(e), :], ssem.at[sl]).start()

    def wait_gather(sl):
        pltpu.make_async_copy(h_hbm.at[pl.ds(0, slab), :], gbufs[sl], gsem.at[sl]).wait()

    def wait_scatter(sl):
        pltpu.make_async_copy(obufs[sl], y_hbm.at[pl.ds(0, slab), :], ssem.at[sl]).wait()

    def looped(fn):
        def body(r, carry):
            fn(r)
            return carry
        lax.fori_loop(0, tile, body, 0, unroll=8)

    def step(sl):
        other = 1 - sl

        @pl.when(i == 0)
        def _():
            obufs[other][...] = jnp.zeros((slab, LANES), jnp.uint32)
            looped(lambda r: gather_row(0, r, sl))

        wait_gather(sl)

        def issue(part):
            lo, hi = part * tile // 4, (part + 1) * tile // 4
            for r in range(lo, hi):
                gather_row(i + 1, r, other)
                scatter_row(i - 1, r, other)

        issue(0)
        for j in range(PACK_ROWS):
            lo, hi = _load_packed(gbufs[sl], j, tile)
            xbuf[:, j * LANES:(j + 1) * LANES] = lo.astype(BF16)
            xbuf[:, (j + PACK_ROWS) * LANES:(j + PACK_ROWS + 1) * LANES] = hi.astype(BF16)
        x = xbuf[...]
        issue(1)
        hg = jnp.dot(x, wg_ref[0], preferred_element_type=F32)
        issue(2)
        hu = jnp.dot(x, wu_ref[0], preferred_element_type=F32)
        he = (_gate_silu(hg) * hu).astype(BF16)
        issue(3)
        y = jnp.dot(he, wd_ref[0], preferred_element_type=F32) * rw_ref[...]

        @pl.when(i >= 1)
        def _():
            wait_scatter(sl)

        _store_packed(obufs[sl], y, tile)

        @pl.when(i == n - 1)
        def _():
            looped(lambda r: scatter_row(i, r, sl))
            wait_scatter(other)
            wait_scatter(sl)
            wait_gather(other)

    @pl.when(i % 2 == 0)
    def _():
        step(0)

    @pl.when(i % 2 == 1)
    def _():
        step(1)


def _experts(h2p, tile_exp, entries, row_w, wg, wu, wd, ntok):
    tile = EXPERT_TILE
    p = entries.shape[0]
    d = wg.shape[1]
    de = wg.shape[2]
    lead = p + jnp.arange(tile, dtype=jnp.int32)
    trail = jnp.full((tile,), TOP_K * ntok, jnp.int32)
    plan = jnp.concatenate([lead, entries, trail])
    slab = pltpu.VMEM((tile * PACK_ROWS, LANES), jnp.uint32)
    kern = functools.partial(_expert_kernel, ntok=ntok, tile=tile)
    return pl.pallas_call(
        kern,
        grid_spec=pltpu.PrefetchScalarGridSpec(
            num_scalar_prefetch=2,
            grid=(p // tile,),
            in_specs=[pl.BlockSpec(memory_space=pl.ANY),
                      pl.BlockSpec((tile, 1), lambda i, te, en: (i, 0)),
                      pl.BlockSpec((1, d, de), lambda i, te, en: (te[i], 0, 0)),
                      pl.BlockSpec((1, d, de), lambda i, te, en: (te[i], 0, 0)),
                      pl.BlockSpec((1, de, d), lambda i, te, en: (te[i], 0, 0))],
            out_specs=pl.BlockSpec(memory_space=pl.ANY),
            scratch_shapes=[slab, slab, pltpu.VMEM((tile, d), BF16), slab, slab,
                            pltpu.SemaphoreType.DMA((2,)),
                            pltpu.SemaphoreType.DMA((2,))]),
        out_shape=jax.ShapeDtypeStruct(((p + tile) * PACK_ROWS, LANES), jnp.uint32),
        compiler_params=_cparams(("arbitrary",)),
        name="experts",
    )(tile_exp, plan, h2p, row_w, wg, wu, wd)


def _route_plan(idx, w, ntok):
    tile = EXPERT_TILE
    n_ent = TOP_K * ntok
    p = n_ent + N_EXPERTS * tile
    flat_e = idx.reshape(-1)
    ent = jnp.arange(n_ent, dtype=jnp.int32)
    order = jnp.sort(flat_e * n_ent + ent) % n_ent
    counts = jnp.sum((flat_e[None, :] == jnp.arange(N_EXPERTS)[:, None]).astype(jnp.int32), axis=1)
    padded = ((counts + tile - 1) // tile) * tile
    gstart = jnp.cumsum(padded) - padded
    gend = gstart + padded
    ustart = jnp.cumsum(counts) - counts
    tstart = jnp.arange(p // tile, dtype=jnp.int32) * tile
    texp = jnp.sum((tstart[:, None] >= gend[None, :]).astype(jnp.int32), axis=1)
    used = tstart < gend[-1]
    texp = jnp.where(used, texp, N_EXPERTS - 1).astype(jnp.int32)
    local = (tstart - gstart[texp])[:, None] + jnp.arange(tile, dtype=jnp.int32)[None, :]
    real = (used[:, None] & (local < counts[texp][:, None])).reshape(p)
    src = order[jnp.clip(ustart[texp][:, None] + local, 0, n_ent - 1).reshape(p)]
    pad_id = n_ent + jnp.cumsum(1 - real.astype(jnp.int32)) - 1
    entries = jnp.where(real, src, pad_id).astype(jnp.int32)
    row_w = jnp.where(real, w.reshape(-1)[src], 0.0).reshape(p, 1)
    return texp, entries, row_w


def _combine_kernel(x_ref, y0_ref, y1_ref, mb_ref, mc_ref, o_ref, *, tile, tpb, lc):
    gate = _row_mod(pl.program_id(0), mb_ref, mc_ref, 5, tile, tpb, lc)
    for j in range(PACK_ROWS):
        lo0, hi0 = _load_packed(y0_ref, j, tile)
        lo1, hi1 = _load_packed(y1_ref, j, tile)
        for jj, y in ((j, lo0 + lo1), (j + PACK_ROWS, hi0 + hi1)):
            sl = slice(jj * LANES, (jj + 1) * LANES)
            o_ref[:, sl] = x_ref[:, sl] + gate[:, sl] * y


def _combine(xn, ybuf, mods, nb, s, lc):
    ntok, d = xn.shape
    tile = s // 8
    tpb = s // tile
    nt = ntok // tile
    kern = functools.partial(_combine_kernel, tile=tile, tpb=tpb, lc=lc)
    return pl.pallas_call(
        kern,
        grid=(nt,),
        in_specs=[pl.BlockSpec((tile, d), lambda i: (i, 0)),
                  pl.BlockSpec((tile * PACK_ROWS, LANES), lambda i: (i, 0)),
                  pl.BlockSpec((tile * PACK_ROWS, LANES), lambda i: (nt + i, 0)),
                  pl.BlockSpec((1, 6, d), lambda i: (i // tpb, 0, 0)),
                  pl.BlockSpec((1, 6, d), lambda i: (nb, 0, 0))],
        out_specs=pl.BlockSpec((tile, d), lambda i: (i, 0)),
        out_shape=jax.ShapeDtypeStruct((ntok, d), F32),
        compiler_params=_cparams(("parallel",)),
        name="combine",
    )(xn, ybuf, ybuf, mods, mods)


def _rope_tables(l, lc, rot_dim, lane_off, group):
    n_freq = rot_dim // 4
    inv = jnp.asarray(ROPE_BASE ** (-np.arange(n_freq, dtype=np.float32) / n_freq), dtype=F32)
    t = jnp.arange(l, dtype=jnp.int32)
    row = (t // GRID_W).astype(F32)
    col = (t % GRID_W).astype(F32)
    ang = jnp.concatenate([row[:, None] * inv, col[:, None] * inv], axis=-1)
    cos, sin = jnp.cos(ang), jnp.sin(ang)
    cos_g = jnp.ones((l, group), F32).at[:, lane_off:lane_off + rot_dim].set(jnp.concatenate([cos, cos], -1))
    sin_g = jnp.zeros((l, group), F32).at[:, lane_off:lane_off + rot_dim].set(jnp.concatenate([-sin, sin], -1))
    reps = LANES // group
    cos_t = jnp.tile(cos_g, (1, reps))
    sin_t = jnp.tile(sin_g, (1, reps))
    cos_t = jnp.concatenate([jnp.ones((lc, LANES), F32), cos_t], axis=0)
    sin_t = jnp.concatenate([jnp.zeros((lc, LANES), F32), sin_t], axis=0)
    return cos_t, sin_t


def _na_bias(rpb, rows):
    nq = rows // NA_QROWS
    qi = np.arange(Q_TILE)
    ki = np.arange(NA_KEYS)
    qr_rel, qc = qi // GRID_W, qi % GRID_W
    kr_rel, kc = ki // GRID_W, ki % GRID_W
    win = np.clip(qc - NA_COLS // 2, 0, GRID_W - NA_COLS)
    cols = np.arange(GRID_W)
    dcol = cols[None, :] - cols[:, None] + NA_COLS - 1
    oh_c = (dcol[None] == np.arange(2 * NA_COLS - 1)[:, None, None]).astype(np.float32)
    by_col = jnp.einsum('hrd,dqk->hrqk', rpb.astype(F32), oh_c, precision=HIGHEST)
    tabs = []
    for t in (0, min(1, nq - 1), nq - 1):
        ks = int(np.clip(t * NA_QROWS - NA_ROWS // 2, 0, rows - NA_KROWS))
        qr = t * NA_QROWS + qr_rel
        kr = ks + kr_rel
        r0 = np.clip(qr - NA_ROWS // 2, 0, rows - NA_ROWS)
        ok = ((kr[None, :] >= r0[:, None]) & (kr[None, :] < r0[:, None] + NA_ROWS)
              & (kc[None, :] >= win[:, None]) & (kc[None, :] < win[:, None] + NA_COLS))
        drow = (ks + np.arange(NA_KROWS))[None, :] - (t * NA_QROWS + np.arange(NA_QROWS))[:, None] + NA_ROWS - 1
        oh_r = (drow[None] == np.arange(2 * NA_ROWS - 1)[:, None, None]).astype(np.float32)
        bias = jnp.einsum('rab,hrqk->haqbk', oh_r, by_col, precision=HIGHEST).reshape(-1, Q_TILE, NA_KEYS)
        tabs.append(jnp.where(jnp.asarray(ok)[None], bias, NEG_INF))
    tabs.append(jnp.full_like(tabs[0], NEG_INF))
    return jnp.stack(tabs)


def _layer_weights(w_in, b_w_q_up, b_w_kv_up, w_branch):
    cuts = np.cumsum([512, 128, 128, 512, 256, 32, 512, 512, 512, 512, 512, 512, 512, 512])
    (a_q, a_k, a_v, b_cq, b_ckv, b_kr, c_q, c_i, c_ff, c_fb, c_g, d_q, d_k, d_v, gates) = jnp.split(w_in, cuts, axis=1)
    d = w_in.shape[0]
    aq = a_q.reshape(d, A_HEADS, HEAD_DIM)
    aq = jnp.stack([aq[:, 0:4], aq[:, 4:8]], axis=2).reshape(d, A_HEADS * HEAD_DIM)
    w = jnp.concatenate([gates, aq, a_k, a_v, b_ckv, b_cq, c_q, c_i, c_ff, c_fb, c_g, d_q, d_k, d_v], axis=1)
    wkr = jnp.zeros((d, LANES), F32).at[:, B_NOPE:B_NOPE + B_ROPE].set(b_kr)
    wq = b_w_q_up.reshape(B_Q_RANK, B_HEADS, B_NOPE + B_ROPE)
    wq = jnp.pad(wq, ((0, 0), (0, 0), (0, LANES - B_NOPE - B_ROPE))).reshape(B_Q_RANK, B_HEADS * LANES)
    wkv = b_w_kv_up.reshape(B_KV_RANK, B_HEADS, B_NOPE + B_V)
    wk = jnp.pad(wkv[:, :, :B_NOPE], ((0, 0), (0, 0), (0, LANES - B_NOPE))).reshape(B_KV_RANK, B_HEADS * LANES)
    wv = wkv[:, :, B_NOPE:].reshape(B_KV_RANK, B_HEADS * B_V)
    wb0 = w_branch[0].reshape(A_HEADS, HEAD_DIM, -1)
    wb0 = jnp.stack([wb0[0:4], wb0[4:8]], axis=1).reshape(A_HEADS * HEAD_DIM, -1)
    wb = jnp.concatenate([wb0[None], w_branch[1:]], axis=0)
    cast = lambda t: t.astype(BF16)
    return cast(w), cast(wkr), cast(wq), cast(wk), cast(wv), cast(wb)


def _group_matrix(bounds):
    gid = np.zeros((LANES,), np.int32)
    size = np.zeros((LANES,), np.float32)
    for g, (lo, hi) in enumerate(bounds):
        gid[lo:hi] = g
        size[lo:hi] = hi - lo
    gmat = (gid[:, None] == gid[None, :]).astype(np.float32)
    return jnp.asarray(gmat, dtype=BF16), jnp.asarray(1.0 / size, dtype=F32).reshape(1, LANES)


def _pad_lanes(v, off=0):
    return jnp.zeros((1, LANES), F32).at[0, off:off + v.shape[0]].set(v.astype(F32))


def kernel(x, c, ctx, c_ctx, w_mod, b_mod, g_norm_mix, g_norm_ffn, w_in, a_gq, a_gk, a_sink, b_g_qa, b_g_kva,
           b_w_q_up, b_w_kv_up, b_gq, b_gk, c_lb_logits, c_g_out, d_gq, d_gk, d_rpb, w_branch, w_out,
           router_w, router_b, w_exp_gate, w_exp_up, w_exp_down):
    nb, l, d = x.shape
    lc = ctx.shape[1]
    s = lc + l
    ntok = nb * s
    depth = w_in.shape[0]
    assert l % Q_TILE == 0 and lc % Q_TILE == 0 and (l // GRID_W) >= NA_KROWS and s % 64 == 0

    xs = jnp.concatenate([ctx, x], axis=1).reshape(ntok, d)

    mod_rows = 16
    cvecs = jnp.zeros((mod_rows, d), F32).at[:nb].set(c).at[nb].set(c_ctx)
    mods = _adaln(cvecs, w_mod, b_mod).reshape(depth, mod_rows, 6, d)

    cos_a, sin_a = _rope_tables(l, lc, HEAD_DIM, 0, HEAD_DIM)
    cos_b, sin_b = _rope_tables(l, lc, B_ROPE, B_NOPE, LANES)
    gmat64, _ = _group_matrix([(0, 64), (64, 128)])
    gmat_b, inv_b = _group_matrix([(0, B_NOPE), (B_NOPE, B_NOPE + B_ROPE), (B_NOPE + B_ROPE, LANES)])
    lb_p = jax.nn.softmax(c_lb_logits.astype(F32), axis=1)
    lb_all = jnp.clip(jnp.cumsum(lb_p, axis=1) - lb_p[:, :1], 0.0, 1.0)
    rw_t = router_w.astype(F32).T
    rb = router_b.astype(F32).reshape(N_EXPERTS, 1)

    for lyr in range(depth):
        w, wkr, wq, wk, wv, wb = _layer_weights(w_in[lyr], b_w_q_up[lyr], b_w_kv_up[lyr], w_branch[lyr])
        m = mods[lyr]
        proj, kr = _in_proj(xs, m, g_norm_mix[lyr], w, wkr, nb, s, lc)

        two = lambda g: jnp.concatenate([g, g]).astype(F32)
        gvecs = jnp.stack([two(a_gq[lyr]), two(a_gk[lyr]), two(d_gq[lyr]), two(d_gk[lyr])])
        qk_ad = _prep_ad(proj, cos_a, sin_a, gvecs, gmat64, s)
        qb, kb, vb = _prep_b(proj, kr, wq, wk, wv, b_g_qa[lyr].reshape(1, -1), b_g_kva[lyr].reshape(1, -1),
                             _pad_lanes(b_gq[lyr]), _pad_lanes(b_gk[lyr]), gmat_b, inv_b, cos_b, sin_b, s)

        oa = _attn_a(a_sink[lyr].astype(F32), qk_ad, proj, nb, s, lc)
        ob = _attn_b(qb, kb, vb, nb, s, lc)
        ofw, obw = _gla(proj, lb_all[:, lyr], nb, s, lc)
        od = _attn_d(qk_ad, proj, _na_bias(d_rpb[lyr], l // GRID_W), nb, s, lc)

        y = _branch_mix(oa, ob, ofw, obw, od, proj, wb, c_g_out[lyr].reshape(1, -1).astype(F32), s)
        xn, h2c, h2p = _out_proj(y, xs, m, w_out[lyr].astype(BF16), g_norm_ffn[lyr].reshape(1, -1), nb, s, lc)

        idx, wts = _router(h2c, rw_t, rb)
        texp, entries, row_w = _route_plan(idx, wts, ntok)
        ybuf = _experts(h2p, texp, entries, row_w, w_exp_gate[lyr].astype(BF16),
                        w_exp_up[lyr].astype(BF16), w_exp_down[lyr].astype(BF16), ntok)
        xs = _combine(xn, ybuf, m, nb, s, lc)

    return xs.reshape(nb, s, d)[:, lc:]
```
